```python
import math
import jax, jax.numpy as jnp
from jax import lax
import numpy as np

D_MODEL = 2048
BATCH = 16
SEQ = 256
DEPTH = 2
DEC_BATCH = 2
DEC_SEQ = 2048
PAST_LEN = 512

GRID_W = 64
POOL_GROUPS = 4
POOL_GC = 128
POOL_W = POOL_GROUPS * POOL_GC
POOL_WINDOWS = (2, 4, 8, 16)
GDN_HEADS = 8
GDN_HEAD_DIM = 128
GDN_W = GDN_HEADS * GDN_HEAD_DIM
SHORT_CONV = 3
GDN_CHUNK = 64
NA_HEADS = 8
NA_HEAD_DIM = 64
NA_W = NA_HEADS * NA_HEAD_DIM
NA_WR = 8
NA_WC = 16
Q_BLOCK = 128
D_FF = 5632
FFN_CONV = 3
N_IN = POOL_W + 4 * GDN_W + 4 * GDN_HEADS + 3 * NA_W + 3 * D_MODEL
EPS = 1e-6

kernel_name = 'hybrid_pool_gdn_natten_prefix_dit_step'

f32 = jnp.float32


def rmsnorm(x, g):
    xf = x.astype(f32)
    y = xf * lax.rsqrt(jnp.mean(xf * xf, axis=-1, keepdims=True) + EPS)
    return (y * g.astype(f32)).astype(x.dtype)


def l2norm(x):
    return x * lax.rsqrt(jnp.sum(x * x, axis=-1, keepdims=True) + EPS)


def dwconv_centred(x, w):
    k, ch = w.shape
    return lax.conv_general_dilated(x, w[:, None, :].astype(x.dtype), (1,), [(k // 2, k // 2)],
                                    dimension_numbers=('NWC', 'WIO', 'NWC'), feature_group_count=ch)


def modulation(cvec, w, b):
    m = (jax.nn.silu(cvec) @ w + b).reshape(-1, 1, 6 * D_MODEL)
    return jnp.split(m, 6, axis=-1)


def multiscale_pool(u, pool_w, pool_scale):
    B, T, _ = u.shape
    ug = u.astype(f32).reshape(B, T, POOL_GROUPS, POOL_GC)
    csum = jnp.concatenate([jnp.zeros((B, 1, POOL_GROUPS, POOL_GC), f32), jnp.cumsum(ug, axis=1)], axis=1)
    t = np.arange(T)[:, None]
    win = np.array(POOL_WINDOWS)[None, :]
    lo = np.maximum(t - win // 2, 0)
    hi = np.minimum(t + win - 1 - win // 2, T - 1)
    gidx = np.arange(POOL_GROUPS)[None, :]
    wsum = csum[:, hi + 1, gidx] - csum[:, lo, gidx]
    mean = wsum / (hi - lo + 1).astype(np.float32)[None, :, :, None]
    y = jnp.einsum('btgc,gcd->btgd', mean - ug, pool_w.astype(f32)) * pool_scale.astype(f32).reshape(POOL_GROUPS, POOL_GC)
    return y.reshape(B, T, POOL_W).astype(u.dtype)


def chunk_gated_delta(q, k, v, g, beta, s0):
    B, H, T, Dh = q.shape
    C = GDN_CHUNK
    N = T // C
    q, k, v = (a.reshape(B, H, N, C, Dh) for a in (q, k, v))
    g = jnp.cumsum(g.reshape(B, H, N, C), axis=-1)
    beta = beta.reshape(B, H, N, C)
    kb = k * beta[..., None]
    incl = np.tril(np.ones((C, C), bool))
    strict = np.tril(np.ones((C, C), bool), -1)
    decay = jnp.exp(jnp.where(incl, g[..., :, None] - g[..., None, :], -jnp.inf))
    lmat = jnp.where(strict, jnp.einsum('bhnid,bhnjd->bhnij', kb, k) * decay, 0.0)
    eye = jnp.broadcast_to(jnp.eye(C, dtype=f32), lmat.shape)
    tinv = lax.linalg.triangular_solve(lmat + eye, eye, left_side=True, lower=True, unit_diagonal=True)
    u = tinv @ (v * beta[..., None])
    w = tinv @ (kb * jnp.exp(g)[..., None])
    qk = jnp.einsum('bhnid,bhnjd->bhnij', q, k) * decay
    qg = q * jnp.exp(g)[..., None]
    kd = k * jnp.exp(g[..., -1:] - g)[..., None]
    glast = jnp.exp(g[..., -1])
    xs = tuple(jnp.moveaxis(a, 2, 0) for a in (u, w, qk, qg, kd, glast))

    def step(S, inp):
        u_n, w_n, qk_n, qg_n, kd_n, gl_n = inp
        v_new = u_n - w_n @ S
        o = qg_n @ S + qk_n @ v_new
        S = S * gl_n[..., None, None] + jnp.swapaxes(kd_n, -1, -2) @ v_new
        return S, o

    S, o = lax.scan(step, s0, xs)
    return jnp.moveaxis(o, 0, 2).reshape(B, H, T, Dh), S


def gdn_branch(u_qkv, u_z, u_beta, u_a, conv_w, a_log, dt_bias, norm_g, s0):
    B, T, _ = u_qkv.shape
    H, Dh = GDN_HEADS, GDN_HEAD_DIM
    qkv = jax.nn.silu(dwconv_centred(u_qkv, conv_w)).astype(f32)
    to_bhtd = lambda a: a.reshape(B, T, H, Dh).transpose(0, 2, 1, 3)
    q, k, v = (to_bhtd(a) for a in jnp.split(qkv, 3, axis=-1))
    q = l2norm(q) * (Dh ** -0.5)
    k = l2norm(k)
    beta = jax.nn.sigmoid(u_beta.astype(f32)).reshape(B, T, 2, H).transpose(2, 0, 3, 1)
    a_in = u_a.astype(f32).reshape(B, T, 2, H).transpose(2, 0, 3, 1)
    g = -jnp.exp(a_log.astype(f32))[:, None, :, None] * jax.nn.softplus(a_in + dt_bias.astype(f32)[:, None, :, None])
    o_f, s_f = chunk_gated_delta(q, k, v, g[0], beta[0], s0[:, 0])
    flip = lambda a: jnp.flip(a, axis=2)
    o_b, s_b = chunk_gated_delta(flip(q), flip(k), flip(v), flip(g[1]), flip(beta[1]), s0[:, 1])
    o = (o_f + flip(o_b)).transpose(0, 2, 1, 3)
    o = rmsnorm(o, norm_g) * jax.nn.silu(u_z.astype(f32).reshape(B, T, H, Dh))
    return o.reshape(B, T, GDN_W).astype(u_qkv.dtype), jnp.stack([s_f, s_b], axis=1)


def attend_context(q, k, v):
    B, S, H, Dh = q.shape
    qb = q.reshape(B, S // Q_BLOCK, Q_BLOCK, H, Dh).swapaxes(0, 1)

    def block(qi):
        s = jnp.einsum('bqhd,bkhd->bhqk', qi, k).astype(f32) * (Dh ** -0.5)
        p = jax.nn.softmax(s, axis=-1).astype(v.dtype)
        return jnp.einsum('bhqk,bkhd->bqhd', p, v)

    o = lax.map(block, qb)
    return o.swapaxes(0, 1).reshape(B, S, H * Dh)


def na_latent(q, k, v, k_ctx, v_ctx, rpb):
    B, T, H, Dh = q.shape
    rows = T // GRID_W
    wr = min(NA_WR, rows)
    r = np.arange(rows)
    key_rows = np.clip(r - wr // 2, 0, rows - wr)[:, None] + np.arange(wr)[None, :]
    col = np.arange(GRID_W)
    col_start = np.clip(col - NA_WC // 2, 0, GRID_W - NA_WC)
    col_mask = (col[None, :] >= col_start[:, None]) & (col[None, :] < col_start[:, None] + NA_WC)
    d_row = key_rows - r[:, None] + NA_WR - 1
    d_col = np.clip(col[None, :] - col[:, None], -(NA_WC - 1), NA_WC - 1) + NA_WC - 1
    bias = jnp.take(rpb[:, d_row], d_col, axis=-1)
    bias = bias.transpose(0, 1, 3, 2, 4).astype(f32)
    qg = q.reshape(B, rows, GRID_W, H, Dh)
    kb = k.reshape(B, rows, GRID_W, H, Dh)[:, key_rows]
    vb = v.reshape(B, rows, GRID_W, H, Dh)[:, key_rows]
    scale = Dh ** -0.5
    s_nb = jnp.einsum('brqhd,brikhd->bhrqik', qg, kb).astype(f32) * scale + bias[None]
    s_nb = jnp.where(col_mask[None, None, None, :, None, :], s_nb, -jnp.inf)
    s_ctx = jnp.einsum('brqhd,bshd->bhrqs', qg, k_ctx).astype(f32) * scale
    n_nb = wr * GRID_W
    p = jax.nn.softmax(jnp.concatenate([s_nb.reshape(B, H, rows, GRID_W, n_nb), s_ctx], axis=-1), axis=-1).astype(v.dtype)
    p_nb = p[..., :n_nb].reshape(B, H, rows, GRID_W, wr, GRID_W)
    o = jnp.einsum('bhrqik,brikhd->brqhd', p_nb, vb) + jnp.einsum('bhrqs,bshd->brqhd', p[..., n_nb:], v_ctx)
    return o.reshape(B, T, H * Dh)


def token_mixer(h, lp, ctx):
    B, T, _ = h.shape
    sizes = (POOL_W, 3 * GDN_W, GDN_W, 2 * GDN_HEADS, 2 * GDN_HEADS, 3 * NA_W)
    u_pool, u_qkv, u_z, u_beta, u_a, u_na, u_gate = jnp.split(h @ lp['w_in'], np.cumsum(sizes).tolist(), axis=-1)
    y_pool = multiscale_pool(u_pool, lp['pool_w'], lp['pool_scale'])
    if ctx is None:
        s0 = jnp.zeros((B, 2, GDN_HEADS, GDN_HEAD_DIM, GDN_HEAD_DIM), f32)
    else:
        s0 = ctx[2].astype(f32)
    y_gdn, s_fin = gdn_branch(u_qkv, u_z, u_beta, u_a, lp['gdn_conv'], lp['gdn_a_log'], lp['gdn_dt_bias'], lp['gdn_norm_g'], s0)
    q, k, v = (a.reshape(B, T, NA_HEADS, NA_HEAD_DIM) for a in jnp.split(u_na, 3, axis=-1))
    if ctx is None:
        y_na = attend_context(q, k, v)
    else:
        y_na = na_latent(q, k, v, ctx[0], ctx[1], lp['na_rpb'])
    gate = jax.nn.sigmoid(u_gate).reshape(B, T, 3, D_MODEL)
    merged = (gate[:, :, 0] * (y_pool @ lp['w_branch_pool'])
              + gate[:, :, 1] * (y_gdn @ lp['w_branch_gdn'])
              + gate[:, :, 2] * (y_na @ lp['w_branch_na']))
    return merged @ lp['w_out'], (k, v, s_fin)


def conv_ffn(h, w_up, conv_w, w_down):
    a, b = jnp.split(dwconv_centred(h @ w_up, conv_w), 2, axis=-1)
    return (jax.nn.silu(a) * b) @ w_down


def trunk_layer(x, mods, lp, ctx):
    sh1, sc1, g1, sh2, sc2, g2 = mods
    h = rmsnorm(x, lp['g_norm1']) * (1 + sc1) + sh1
    mix, ctx_out = token_mixer(h, lp, ctx)
    x = x + g1 * mix
    h = rmsnorm(x, lp['g_norm2']) * (1 + sc2) + sh2
    x = x + g2 * conv_ffn(h, lp['w_up'], lp['ffn_conv'], lp['w_down'])
    return x, ctx_out


def setup_inputs(seed: int = 0) -> dict:
    key = jax.random.key(seed)
    ks = jax.random.split(key, 32)
    nrm = lambda kk, shape, s: jax.random.normal(kk, shape, f32) * s
    L, D = DEPTH, D_MODEL
    dt = jnp.exp(jax.random.uniform(ks[15], (L, 2, GDN_HEADS), f32, math.log(1e-3), math.log(1e-1)))
    return {
        'x_prompt': nrm(ks[0], (BATCH, SEQ, D), 1.0),
        'x_sample': nrm(ks[1], (DEC_BATCH, DEC_SEQ, D), 1.0),
        'cache_na_k': nrm(ks[2], (DEC_BATCH, L, PAST_LEN, NA_HEADS, NA_HEAD_DIM), 1.0),
        'cache_na_v': nrm(ks[3], (DEC_BATCH, L, PAST_LEN, NA_HEADS, NA_HEAD_DIM), 1.0),
        'state_gdn': nrm(ks[4], (DEC_BATCH, L, 2, GDN_HEADS, GDN_HEAD_DIM, GDN_HEAD_DIM), 0.1),
        'c': nrm(ks[5], (DEC_BATCH, D), 1.0),
        'c_ctx': nrm(ks[6], (D,), 1.0),
        'w_ada': nrm(ks[7], (L, D, 6 * D), 0.5 * D ** -0.5),
        'b_ada': nrm(ks[8], (L, 6 * D), 0.01),
        'g_norm1': 1.0 + nrm(ks[9], (L, D), 0.02),
        'w_in': nrm(ks[10], (L, D, N_IN), D ** -0.5),
        'pool_w': nrm(ks[11], (L, POOL_GROUPS, POOL_GC, POOL_GC), POOL_GC ** -0.5),
        'pool_scale': 1.0 + nrm(ks[12], (L, POOL_W), 0.1),
        'gdn_conv': nrm(ks[13], (L, SHORT_CONV, 3 * GDN_W), SHORT_CONV ** -0.5),
        'gdn_a_log': jnp.log(jax.random.uniform(ks[14], (L, 2, GDN_HEADS), f32, 1.0, 16.0)),
        'gdn_dt_bias': dt + jnp.log(-jnp.expm1(-dt)),
        'gdn_norm_g': 1.0 + nrm(ks[16], (L, GDN_HEAD_DIM), 0.02),
        'na_rpb': nrm(ks[17], (L, NA_HEADS, 2 * NA_WR - 1, 2 * NA_WC - 1), 0.1),
        'w_branch_pool': nrm(ks[18], (L, POOL_W, D), POOL_W ** -0.5),
        'w_branch_gdn': nrm(ks[19], (L, GDN_W, D), GDN_W ** -0.5),
        'w_branch_na': nrm(ks[20], (L, NA_W, D), NA_W ** -0.5),
        'w_out': nrm(ks[21], (L, D, D), D ** -0.5),
        'g_norm2': 1.0 + nrm(ks[22], (L, D), 0.02),
        'w_up': nrm(ks[23], (L, D, 2 * D_FF), D ** -0.5),
        'ffn_conv': nrm(ks[24], (L, FFN_CONV, 2 * D_FF), FFN_CONV ** -0.5),
        'w_down': nrm(ks[25], (L, D_FF, D), D_FF ** -0.5),
        'g_final': 1.0 + nrm(ks[26], (D,), 0.02),
    }


def reference(x_prompt, x_sample, cache_na_k, cache_na_v, state_gdn, c, c_ctx, w_ada, b_ada, g_norm1, w_in,
              pool_w, pool_scale, gdn_conv, gdn_a_log, gdn_dt_bias, gdn_norm_g, na_rpb, w_branch_pool,
              w_branch_gdn, w_branch_na, w_out, g_norm2, w_up, ffn_conv, w_down, g_final):
    xp, xs = x_prompt, x_sample
    new_k, new_v, new_s = [], [], []
    for l in range(DEPTH):
        lp = {'g_norm1': g_norm1[l], 'w_in': w_in[l], 'pool_w': pool_w[l], 'pool_scale': pool_scale[l],
              'gdn_conv': gdn_conv[l], 'gdn_a_log': gdn_a_log[l], 'gdn_dt_bias': gdn_dt_bias[l],
              'gdn_norm_g': gdn_norm_g[l], 'na_rpb': na_rpb[l], 'w_branch_pool': w_branch_pool[l],
              'w_branch_gdn': w_branch_gdn[l], 'w_branch_na': w_branch_na[l], 'w_out': w_out[l],
              'g_norm2': g_norm2[l], 'w_up': w_up[l], 'ffn_conv': ffn_conv[l], 'w_down': w_down[l]}
        xp, (k_l, v_l, s_l) = trunk_layer(xp, modulation(c_ctx, w_ada[l], b_ada[l]), lp, None)
        new_k.append(k_l)
        new_v.append(v_l)
        new_s.append(s_l)
        ctx_l = (cache_na_k[:, l], cache_na_v[:, l], state_gdn[:, l])
        xs, _ = trunk_layer(xs, modulation(c, w_ada[l], b_ada[l]), lp, ctx_l)
    y_prompt = rmsnorm(xp, g_final)
    y_sample = rmsnorm(xs, g_final)
    new_cache_na_k = jnp.stack(new_k, axis=1)
    new_cache_na_v = jnp.stack(new_v, axis=1)
    new_state_gdn = jnp.stack(new_s, axis=1).astype(x_prompt.dtype)
    return (y_prompt, y_sample, new_cache_na_k, new_cache_na_v, new_state_gdn)
```

```python
import functools
import math

import numpy as np
import jax
import jax.numpy as jnp
from jax import lax
from jax.experimental import pallas as pl
from jax.experimental.pallas import tpu as pltpu

D_MODEL = 2048
BATCH = 16
SEQ = 256
DEPTH = 2
DEC_BATCH = 2
DEC_SEQ = 2048
PAST_LEN = 512
GRID_W = 64
POOL_GROUPS = 4
POOL_GC = 128
POOL_W = POOL_GROUPS * POOL_GC
POOL_WINDOWS = (2, 4, 8, 16)
GDN_HEADS = 8
GDN_HEAD_DIM = 128
GDN_W = GDN_HEADS * GDN_HEAD_DIM
GDN_CHUNK = 64
NA_HEADS = 8
NA_HEAD_DIM = 64
NA_W = NA_HEADS * NA_HEAD_DIM
NA_WR = 8
NA_WC = 16
D_FF = 5632
EPS = 1e-6

N_CTX = BATCH * SEQ
N_LAT = DEC_BATCH * DEC_SEQ
N_TOK = N_CTX + N_LAT
ROW_GROUP = DEC_SEQ
N_GROUPS = N_TOK // ROW_GROUP
SEQ_TILE = SEQ
HALO = 16
GRID_ROWS = DEC_SEQ // GRID_W
NEG_BIG = -1e30

A_W = POOL_W + 4 * GDN_W
B_W = 128
C_W = 3 * NA_W
G_W = 3 * D_MODEL

VMEM_LIMIT = 48 * 1024 * 1024

f32 = jnp.float32
bf16 = jnp.bfloat16


def _cparams(n_axes):
    return pltpu.CompilerParams(dimension_semantics=("arbitrary",) * n_axes, vmem_limit_bytes=VMEM_LIMIT)


def _dot(a, b):
    return jnp.dot(a.astype(bf16), b.astype(bf16), preferred_element_type=f32)


def _dot_nt(a, b):
    return lax.dot_general(a.astype(bf16), b.astype(bf16), (((1,), (1,)), ((), ())), preferred_element_type=f32)


def _dot_tn(a, b):
    return lax.dot_general(a.astype(bf16), b.astype(bf16), (((0,), (0,)), ((), ())), preferred_element_type=f32)


def _silu(x):
    return x * jax.nn.sigmoid(x)


def _col_tile(n, cap=1536):
    best = 128
    for t in range(128, cap + 1, 128):
        if n % t == 0:
            best = t
    return best


def _mod_kernel(c_ref, w_ref, b_ref, o_ref):
    a = _silu(c_ref[...])
    o_ref[...] = _dot(a, w_ref[...]) + b_ref[...]


def _modulation(cvec, w_ada, b_ada):
    tn = 1024
    n = 6 * D_MODEL
    return pl.pallas_call(
        _mod_kernel,
        grid=(DEPTH, n // tn),
        in_specs=[
            pl.BlockSpec((8, D_MODEL), lambda l, j: (0, 0)),
            pl.BlockSpec((None, D_MODEL, tn), lambda l, j: (l, 0, j)),
            pl.BlockSpec((None, 1, tn), lambda l, j: (l, 0, j)),
        ],
        out_specs=pl.BlockSpec((None, 8, tn), lambda l, j: (l, 0, j)),
        out_shape=jax.ShapeDtypeStruct((DEPTH, 8, n), f32),
        compiler_params=_cparams(2),
        name="modulation",
    )(cvec, w_ada, b_ada.reshape(DEPTH, 1, n))


def _mod_spec(tm, tn, col_of):
    return pl.BlockSpec((None, 1, tn), lambda m, n: ((m * tm) // ROW_GROUP, 0, col_of(n)))


def _norm_mod_kernel(x_ref, g_ref, sc_ref, sh_ref, o_ref):
    x = x_ref[...]
    y = x * lax.rsqrt(jnp.mean(x * x, axis=-1, keepdims=True) + EPS) * g_ref[...]
    o_ref[...] = (y * (1.0 + sc_ref[...]) + sh_ref[...]).astype(o_ref.dtype)


def _norm_mod(x, g, scale, shift):
    tm = 512
    mod = pl.BlockSpec((None, 1, D_MODEL), lambda m: ((m * tm) // ROW_GROUP, 0, 0))
    return pl.pallas_call(
        _norm_mod_kernel,
        grid=(N_TOK // tm,),
        in_specs=[
            pl.BlockSpec((tm, D_MODEL), lambda m: (m, 0)),
            pl.BlockSpec((1, D_MODEL), lambda m: (0, 0)),
            mod, mod,
        ],
        out_specs=pl.BlockSpec((tm, D_MODEL), lambda m: (m, 0)),
        out_shape=jax.ShapeDtypeStruct((N_TOK, D_MODEL), bf16),
        compiler_params=_cparams(1),
        name="norm_mod",
    )(x, g.reshape(1, D_MODEL), scale, shift)


def _mm_kernel(x_ref, w_ref, o_ref):
    o_ref[...] = jnp.dot(x_ref[...], w_ref[...], preferred_element_type=f32).astype(o_ref.dtype)


def _matmul(x, w, out_dtype, name):
    m, k = x.shape
    n = w.shape[1]
    tm = 1024
    tn = _col_tile(n, 1024)
    return pl.pallas_call(
        _mm_kernel,
        grid=(m // tm, n // tn),
        in_specs=[
            pl.BlockSpec((tm, k), lambda i, j: (i, 0)),
            pl.BlockSpec((k, tn), lambda i, j: (0, j)),
        ],
        out_specs=pl.BlockSpec((tm, tn), lambda i, j: (i, j)),
        out_shape=jax.ShapeDtypeStruct((m, n), out_dtype),
        compiler_params=_cparams(2),
        name=name,
    )(x, w)


def _mm_res_kernel(a_ref, w_ref, x_ref, g_ref, o_ref):
    y = jnp.dot(a_ref[...], w_ref[...], preferred_element_type=f32)
    o_ref[...] = x_ref[...] + g_ref[...] * y


def _matmul_residual(a, w, x, gate, name):
    m, k = a.shape
    n = w.shape[1]
    tm = 512
    tn = 512
    return pl.pallas_call(
        _mm_res_kernel,
        grid=(m // tm, n // tn),
        in_specs=[
            pl.BlockSpec((tm, k), lambda i, j: (i, 0)),
            pl.BlockSpec((k, tn), lambda i, j: (0, j)),
            pl.BlockSpec((tm, tn), lambda i, j: (i, j)),
            _mod_spec(tm, tn, lambda j: j),
        ],
        out_specs=pl.BlockSpec((tm, tn), lambda i, j: (i, j)),
        out_shape=jax.ShapeDtypeStruct((m, n), f32),
        compiler_params=_cparams(2),
        name=name,
    )(a, w, x, gate)


def _merge_kernel(yp_ref, yg_ref, yn_ref, wp_ref, wg_ref, wn_ref, g0_ref, g1_ref, g2_ref, o_ref):
    def branch(y_ref, w_ref, g_ref):
        return jax.nn.sigmoid(g_ref[...].astype(f32)) * jnp.dot(y_ref[...], w_ref[...], preferred_element_type=f32)

    acc = branch(yp_ref, wp_ref, g0_ref) + branch(yg_ref, wg_ref, g1_ref) + branch(yn_ref, wn_ref, g2_ref)
    o_ref[...] = acc.astype(o_ref.dtype)


def _merge(y_pool, y_gdn, y_na, w_pool, w_gdn, w_na, u_gate):
    tm = 512
    tn = 512
    nb = D_MODEL // tn
    row = lambda width: pl.BlockSpec((tm, width), lambda i, j: (i, 0))
    wcol = lambda kdim: pl.BlockSpec((kdim, tn), lambda i, j: (0, j))
    gate = lambda b: pl.BlockSpec((tm, tn), lambda i, j: (i, j + b * nb))
    return pl.pallas_call(
        _merge_kernel,
        grid=(N_TOK // tm, nb),
        in_specs=[row(POOL_W), row(GDN_W), row(NA_W), wcol(POOL_W), wcol(GDN_W), wcol(NA_W),
                  gate(0), gate(1), gate(2)],
        out_specs=pl.BlockSpec((tm, tn), lambda i, j: (i, j)),
        out_shape=jax.ShapeDtypeStruct((N_TOK, D_MODEL), bf16),
        compiler_params=_cparams(2),
        name="merge",
    )(y_pool, y_gdn, y_na, w_pool, w_gdn, w_na, u_gate, u_gate, u_gate)


def _tile_seq_len(m):
    return jnp.where(m * SEQ_TILE < N_CTX, SEQ, DEC_SEQ)


def _halo_specs(width, col_of):
    per = SEQ_TILE // HALO
    last = N_TOK // HALO - 1
    main = pl.BlockSpec((SEQ_TILE, width), lambda m, c: (m, col_of(c)))
    prev = pl.BlockSpec((HALO, width), lambda m, c: (jnp.maximum(m * per - 1, 0), col_of(c)))
    nxt = pl.BlockSpec((HALO, width), lambda m, c: (jnp.minimum((m + 1) * per, last), col_of(c)))
    return main, prev, nxt


def _tile_pos(m):
    t_len = _tile_seq_len(m)
    row = lax.broadcasted_iota(jnp.int32, (SEQ_TILE, 1), 0)
    return (m * SEQ_TILE + row) & (t_len - 1), t_len


def _pool_kernel(x_ref, xp_ref, xn_ref, w_ref, s_ref, o_ref, ext_ref):
    m = pl.program_id(0)
    pos, t_len = _tile_pos(m)
    first = ((m * SEQ_TILE) & (t_len - 1)) == 0
    last = (((m + 1) * SEQ_TILE) & (t_len - 1)) == 0
    x = x_ref[...].astype(f32)
    ext_ref[0:HALO, :] = jnp.where(first, 0.0, xp_ref[...].astype(f32))
    ext_ref[HALO:HALO + SEQ_TILE, :] = x
    ext_ref[HALO + SEQ_TILE:, :] = jnp.where(last, 0.0, xn_ref[...].astype(f32))
    outs = []
    for g, win in enumerate(POOL_WINDOWS):
        cols = slice(g * POOL_GC, (g + 1) * POOL_GC)
        acc = jnp.zeros((SEQ_TILE, POOL_GC), f32)
        for s in range(-(win // 2), win - win // 2):
            acc = acc + ext_ref[HALO + s:HALO + s + SEQ_TILE, cols]
        lo = jnp.maximum(pos - win // 2, 0)
        hi = jnp.minimum(pos + win - 1 - win // 2, t_len - 1)
        mean = acc / (hi - lo + 1).astype(f32)
        y = _dot(mean - x[:, cols], w_ref[g]) * s_ref[:, cols]
        outs.append(y)
    o_ref[...] = jnp.concatenate(outs, axis=1).astype(o_ref.dtype)


def _pool(u_a, pool_w, pool_scale):
    main, prev, nxt = _halo_specs(POOL_W, lambda c: 0)
    return pl.pallas_call(
        _pool_kernel,
        grid=(N_TOK // SEQ_TILE, 1),
        in_specs=[main, prev, nxt,
                  pl.BlockSpec((POOL_GROUPS, POOL_GC, POOL_GC), lambda m, c: (0, 0, 0)),
                  pl.BlockSpec((1, POOL_W), lambda m, c: (0, 0))],
        out_specs=pl.BlockSpec((SEQ_TILE, POOL_W), lambda m, c: (m, 0)),
        out_shape=jax.ShapeDtypeStruct((N_TOK, POOL_W), bf16),
        scratch_shapes=[pltpu.VMEM((SEQ_TILE + 2 * HALO, POOL_W), f32)],
        compiler_params=_cparams(2),
        name="pool",
    )(u_a, u_a, u_a, pool_w, pool_scale.reshape(1, POOL_W))


def _conv3_tile(x_ref, xp_ref, xn_ref, w_ref, pos, t_len):
    x = x_ref[...].astype(f32)
    row = lax.broadcasted_iota(jnp.int32, (SEQ_TILE, 1), 0)
    x_prev = xp_ref[...].astype(f32)[HALO - 1:HALO, :]
    x_next = xn_ref[...].astype(f32)[0:1, :]
    before = jnp.where(row == 0, x_prev, pltpu.roll(x, 1, 0))
    after = jnp.where(row == SEQ_TILE - 1, x_next, pltpu.roll(x, SEQ_TILE - 1, 0))
    before = jnp.where(pos == 0, 0.0, before)
    after = jnp.where(pos == t_len - 1, 0.0, after)
    return before * w_ref[0:1, :] + x * w_ref[1:2, :] + after * w_ref[2:3, :]


def _conv_glu_kernel(a_ref, ap_ref, an_ref, b_ref, bp_ref, bn_ref, wa_ref, wb_ref, o_ref):
    pos, t_len = _tile_pos(pl.program_id(0))
    a = _conv3_tile(a_ref, ap_ref, an_ref, wa_ref, pos, t_len)
    b = _conv3_tile(b_ref, bp_ref, bn_ref, wb_ref, pos, t_len)
    o_ref[...] = (_silu(a) * b).astype(o_ref.dtype)


def _conv_glu(up, ffn_conv):
    tc = 512
    nb = D_FF // tc
    a_specs = _halo_specs(tc, lambda c: c)
    b_specs = _halo_specs(tc, lambda c: c + nb)
    return pl.pallas_call(
        _conv_glu_kernel,
        grid=(N_TOK // SEQ_TILE, nb),
        in_specs=[*a_specs, *b_specs,
                  pl.BlockSpec((3, tc), lambda m, c: (0, c)),
                  pl.BlockSpec((3, tc), lambda m, c: (0, c + nb))],
        out_specs=pl.BlockSpec((SEQ_TILE, tc), lambda m, c: (m, c)),
        out_shape=jax.ShapeDtypeStruct((N_TOK, D_FF), bf16),
        compiler_params=_cparams(2),
        name="conv_glu",
    )(up, up, up, up, up, up, ffn_conv, ffn_conv)


GATE_LANES = 4 * GDN_HEADS
CHUNKS_PER_TILE = SEQ_TILE // GDN_CHUNK


def _gate_prep_kernel(u_ref, alog_ref, dtb_ref, cols_ref, rows_ref):
    u = u_ref[...]
    lane = lax.broadcasted_iota(jnp.int32, (1, B_W), 1)
    x = u + dtb_ref[...]
    softplus = jnp.maximum(x, 0.0) + jnp.log1p(jnp.exp(-jnp.abs(x)))
    g = -jnp.exp(alog_ref[...]) * softplus
    r = lax.broadcasted_iota(jnp.int32, (SEQ_TILE, SEQ_TILE), 0)
    c = lax.broadcasted_iota(jnp.int32, (SEQ_TILE, SEQ_TILE), 1)
    shift = int(math.log2(GDN_CHUNK))
    same = (r >> shift) == (c >> shift)
    tri_f = jnp.where(same & (c <= r), 1.0, 0.0)
    tri_b = jnp.where(same & (c >= r), 1.0, 0.0)
    gc_f = jnp.dot(tri_f, g, preferred_element_type=f32, precision=lax.Precision.HIGHEST)
    gc_b = jnp.dot(tri_b, g, preferred_element_type=f32, precision=lax.Precision.HIGHEST)
    cols = jnp.where(lane < 2 * GDN_HEADS, jax.nn.sigmoid(u), jnp.where(lane < 3 * GDN_HEADS, gc_f, gc_b))
    cols_ref[...] = cols
    for j in range(CHUNKS_PER_TILE):
        t = cols[j * GDN_CHUNK:(j + 1) * GDN_CHUNK, :].T
        rows_ref[j] = t[0:GATE_LANES, :]


def _gate_prep(u_b, a_log, dt_bias):
    pad = lambda p: jnp.zeros((1, B_W), f32).at[0, 2 * GDN_HEADS:GATE_LANES].set(p.reshape(-1))
    n_tiles = N_TOK // SEQ_TILE
    return pl.pallas_call(
        _gate_prep_kernel,
        grid=(n_tiles,),
        in_specs=[pl.BlockSpec((SEQ_TILE, B_W), lambda m: (m, 0)),
                  pl.BlockSpec((1, B_W), lambda m: (0, 0)),
                  pl.BlockSpec((1, B_W), lambda m: (0, 0))],
        out_specs=[pl.BlockSpec((SEQ_TILE, B_W), lambda m: (m, 0)),
                   pl.BlockSpec((CHUNKS_PER_TILE, GATE_LANES, GDN_CHUNK), lambda m: (m, 0, 0))],
        out_shape=[jax.ShapeDtypeStruct((N_TOK, B_W), f32),
                   jax.ShapeDtypeStruct((N_TOK // GDN_CHUNK, GATE_LANES, GDN_CHUNK), f32)],
        compiler_params=_cparams(1),
        name="gdn_gate_prep",
    )(u_b, pad(a_log), pad(dt_bias))


def _gdn_conv_silu(x_ref, w_ref, t_len):
    x = x_ref[...].astype(f32)
    row = lax.broadcasted_iota(jnp.int32, (t_len, 1), 0)
    before = jnp.where(row == 0, 0.0, pltpu.roll(x, 1, 0))
    after = jnp.where(row == t_len - 1, 0.0, pltpu.roll(x, t_len - 1, 0))
    return _silu(before * w_ref[0:1, :] + x * w_ref[1:2, :] + after * w_ref[2:3, :])


def _l2norm(x):
    return x * lax.rsqrt(jnp.sum(x * x, axis=-1, keepdims=True) + EPS)


def _gdn_chunk(c, direction, head, q_s, k_s, v_s, gcol_ref, grow_ref, s_ref, o_ref):
    cs = GDN_CHUNK
    r0 = pl.multiple_of(c * cs, cs)
    q = q_s[pl.ds(r0, cs), :]
    k = k_s[pl.ds(r0, cs), :]
    v = v_s[pl.ds(r0, cs), :]
    gates = gcol_ref[pl.ds(r0, cs), :]
    lane = lax.broadcasted_iota(jnp.int32, (1, B_W), 1)
    pick = lambda idx: jnp.sum(jnp.where(lane == idx, gates, 0.0), axis=1, keepdims=True)
    beta = pick(direction * GDN_HEADS + head)
    dec_lane = (2 + direction) * GDN_HEADS + head
    gc = pick(dec_lane)
    gr = grow_ref[c, pl.ds(dec_lane, 1), :]
    ri = lax.broadcasted_iota(jnp.int32, (cs, cs), 0)
    ci = lax.broadcasted_iota(jnp.int32, (cs, cs), 1)
    if direction == 0:
        incl, strict = ri >= ci, ri > ci
        g_tot = gc[cs - 1:cs, :]
    else:
        incl, strict = ri <= ci, ri < ci
        g_tot = gc[0:1, :]
    decay = jnp.exp(jnp.where(incl, gc - gr, -jnp.inf))
    kk = _dot_nt(k, k)
    qk = _dot_nt(q, k) * decay
    lmat = jnp.where(strict, kk * decay * beta, 0.0)
    blk = ri ^ ci
    m_acc = -jnp.where(blk < 2, lmat, 0.0)
    for level in range(1, int(math.log2(cs))):
        c_k = jnp.where((blk >= (1 << level)) & (blk < (2 << level)), lmat, 0.0)
        y = c_k + _dot(c_k, m_acc)
        m_acc = m_acc - y - _dot(m_acc, y)
    e_gc = jnp.exp(gc)
    vb = v * beta
    kbg = k * (beta * e_gc)
    uw = _dot(m_acc, jnp.concatenate([vb, kbg], axis=1))
    u = vb + uw[:, 0:GDN_HEAD_DIM]
    w = kbg + uw[:, GDN_HEAD_DIM:]
    qg = q * e_gc
    kd = k * jnp.exp(g_tot - gc)
    state = s_ref[direction]
    ws = _dot(jnp.concatenate([w, qg], axis=0), state)
    v_new = u - ws[0:cs]
    o_ref[direction, pl.ds(r0, cs), :] = ws[cs:] + _dot(qk, v_new)
    s_ref[direction] = state * jnp.exp(g_tot) + _dot_tn(kd, v_new)


def _gdn_kernel(q_ref, k_ref, v_ref, z_ref, wq_ref, wk_ref, wv_ref, gcol_ref, grow_ref, ng_ref, s0_ref,
                y_ref, sfin_ref, q_s, k_s, v_s, o_s, st_s, *, t_len):
    head = pl.program_id(1)
    n_chunks = t_len // GDN_CHUNK
    q_s[...] = _l2norm(_gdn_conv_silu(q_ref, wq_ref, t_len)) * (GDN_HEAD_DIM ** -0.5)
    k_s[...] = _l2norm(_gdn_conv_silu(k_ref, wk_ref, t_len))
    v_s[...] = _gdn_conv_silu(v_ref, wv_ref, t_len)
    st_s[...] = s0_ref[...]

    def body(i, carry):
        _gdn_chunk(i, 0, head, q_s, k_s, v_s, gcol_ref, grow_ref, st_s, o_s)
        _gdn_chunk(n_chunks - 1 - i, 1, head, q_s, k_s, v_s, gcol_ref, grow_ref, st_s, o_s)
        return carry

    lax.fori_loop(0, n_chunks, body, 0)
    sfin_ref[...] = st_s[...]
    o = o_s[0] + o_s[1]
    o = o * lax.rsqrt(jnp.mean(o * o, axis=-1, keepdims=True) + EPS) * ng_ref[...]
    y_ref[...] = (o * _silu(z_ref[...].astype(f32))).astype(y_ref.dtype)


def _gdn(u_a, gcols, grows, conv_w, norm_g, s0, t_len, n_seq, row_off):
    hd = GDN_HEAD_DIM
    nh = GDN_HEADS
    blk0 = row_off // t_len
    c_q = POOL_W // hd
    col = lambda base: pl.BlockSpec((t_len, hd), lambda s, h: (blk0 + s, base + h))
    cw = lambda base: pl.BlockSpec((3, hd), lambda s, h: (0, base + h))
    n_chunks = t_len // GDN_CHUNK
    state_spec = pl.BlockSpec((None, 2, None, hd, hd), lambda s, h: (s, 0, h, 0, 0))
    return pl.pallas_call(
        functools.partial(_gdn_kernel, t_len=t_len),
        grid=(n_seq, nh),
        in_specs=[col(c_q), col(c_q + nh), col(c_q + 2 * nh), col(c_q + 3 * nh),
                  cw(0), cw(nh), cw(2 * nh),
                  pl.BlockSpec((t_len, B_W), lambda s, h: (blk0 + s, 0)),
                  pl.BlockSpec((n_chunks, GATE_LANES, GDN_CHUNK), lambda s, h: (blk0 + s, 0, 0)),
                  pl.BlockSpec((1, hd), lambda s, h: (0, 0)),
                  state_spec],
        out_specs=[pl.BlockSpec((t_len, hd), lambda s, h: (s, h)), state_spec],
        out_shape=[jax.ShapeDtypeStruct((n_seq * t_len, GDN_W), bf16),
                   jax.ShapeDtypeStruct((n_seq, 2, nh, hd, hd), f32)],
        scratch_shapes=[pltpu.VMEM((t_len, hd), f32)] * 3
        + [pltpu.VMEM((2, t_len, hd), f32), pltpu.VMEM((2, hd, hd), f32)],
        compiler_params=_cparams(2),
        name=f"gdn_t{t_len}",
    )(u_a, u_a, u_a, u_a, conv_w, conv_w, conv_w, gcols, grows, norm_g.reshape(1, hd), s0)


def _ctx_attn_kernel(x_ref, o_ref):
    hd = NA_HEAD_DIM
    outs = []
    for h in range(NA_HEADS):
        q = x_ref[:, h * hd:(h + 1) * hd]
        k = x_ref[:, NA_W + h * hd:NA_W + (h + 1) * hd]
        v = x_ref[:, 2 * NA_W + h * hd:2 * NA_W + (h + 1) * hd]
        s = _dot_nt(q, k) * (hd ** -0.5)
        p = jnp.exp(s - jnp.max(s, axis=-1, keepdims=True))
        outs.append(_dot(p, v) / jnp.sum(p, axis=-1, keepdims=True))
    o_ref[...] = jnp.concatenate(outs, axis=1).astype(o_ref.dtype)


def _ctx_attention(u_c):
    return pl.pallas_call(
        _ctx_attn_kernel,
        grid=(BATCH,),
        in_specs=[pl.BlockSpec((SEQ, C_W), lambda b: (b, 0))],
        out_specs=pl.BlockSpec((SEQ, NA_W), lambda b: (b, 0)),
        out_shape=jax.ShapeDtypeStruct((N_CTX, NA_W), bf16),
        compiler_params=_cparams(1),
        name="ctx_attention",
    )(u_c)


HEADS_PER_STEP = 128 // NA_HEAD_DIM
NB_KEYS = NA_WR * GRID_W


def _na_kernel(q_ref, k_ref, v_ref, kc_ref, vc_ref, bias_ref, o_ref):
    hd = NA_HEAD_DIM
    scale = hd ** -0.5

    def body(r, carry):
        start = jnp.clip(r - NA_WR // 2, 0, GRID_ROWS - NA_WR)
        case = start - r + NA_WR - 1
        q0 = pl.multiple_of(r * GRID_W, GRID_W)
        k0 = pl.multiple_of(start * GRID_W, GRID_W)
        outs = []
        for hh in range(HEADS_PER_STEP):
            cols = slice(hh * hd, (hh + 1) * hd)
            q = q_ref[pl.ds(q0, GRID_W), cols]
            s_nb = _dot_nt(q, k_ref[pl.ds(k0, NB_KEYS), cols]) * scale + bias_ref[hh, case]
            s_cx = _dot_nt(q, kc_ref[:, cols]) * scale
            mx = jnp.maximum(jnp.max(s_nb, axis=-1, keepdims=True), jnp.max(s_cx, axis=-1, keepdims=True))
            p_nb = jnp.exp(s_nb - mx)
            p_cx = jnp.exp(s_cx - mx)
            den = jnp.sum(p_nb, axis=-1, keepdims=True) + jnp.sum(p_cx, axis=-1, keepdims=True)
            o = _dot(p_nb, v_ref[pl.ds(k0, NB_KEYS), cols]) + _dot(p_cx, vc_ref[:, cols])
            outs.append(o / den)
        o_ref[pl.ds(q0, GRID_W), :] = jnp.concatenate(outs, axis=1).astype(o_ref.dtype)
        return carry

    lax.fori_loop(0, GRID_ROWS, body, 0)


def _na_bias_tiles(rpb):
    col = np.arange(GRID_W)
    col_start = np.clip(col - NA_WC // 2, 0, GRID_W - NA_WC)
    col_mask = (col[None, :] >= col_start[:, None]) & (col[None, :] < col_start[:, None] + NA_WC)
    d_col = np.clip(col[None, :] - col[:, None], -(NA_WC - 1), NA_WC - 1) + NA_WC - 1
    table = jnp.where(col_mask[None, None], rpb[:, :, d_col], NEG_BIG)
    tiles = [table[:, case:case + NA_WR].transpose(0, 2, 1, 3).reshape(NA_HEADS, GRID_W, NB_KEYS)
             for case in range(NA_WR)]
    return jnp.stack(tiles, axis=1).astype(f32)


def _na_attention(u_c, k_ctx, v_ctx, bias_tiles):
    blk0 = N_CTX // DEC_SEQ
    nhp = NA_HEADS // HEADS_PER_STEP
    col = lambda base: pl.BlockSpec((DEC_SEQ, 128), lambda b, p: (blk0 + b, base + p))
    cache = pl.BlockSpec((None, PAST_LEN, 128), lambda b, p: (b, 0, p))
    return pl.pallas_call(
        _na_kernel,
        grid=(DEC_BATCH, nhp),
        in_specs=[col(0), col(nhp), col(2 * nhp), cache, cache,
                  pl.BlockSpec((HEADS_PER_STEP, NA_WR, GRID_W, NB_KEYS), lambda b, p: (p, 0, 0, 0))],
        out_specs=pl.BlockSpec((DEC_SEQ, 128), lambda b, p: (b, p)),
        out_shape=jax.ShapeDtypeStruct((N_LAT, NA_W), bf16),
        compiler_params=_cparams(2),
        name="na_attention",
    )(u_c, u_c, u_c, k_ctx, v_ctx, bias_tiles)


def _final_norm_kernel(x_ref, g_ref, o_ref):
    x = x_ref[...]
    o_ref[...] = x * lax.rsqrt(jnp.mean(x * x, axis=-1, keepdims=True) + EPS) * g_ref[...]


def _final_norm(x, g, n_rows, row_off):
    tm = 512
    blk0 = row_off // tm
    return pl.pallas_call(
        _final_norm_kernel,
        grid=(n_rows // tm,),
        in_specs=[pl.BlockSpec((tm, D_MODEL), lambda m: (blk0 + m, 0)),
                  pl.BlockSpec((1, D_MODEL), lambda m: (0, 0))],
        out_specs=pl.BlockSpec((tm, D_MODEL), lambda m: (m, 0)),
        out_shape=jax.ShapeDtypeStruct((n_rows, D_MODEL), f32),
        compiler_params=_cparams(1),
        name="final_norm",
    )(x, g.reshape(1, D_MODEL))


def kernel(x_prompt, x_sample, cache_na_k, cache_na_v, state_gdn, c, c_ctx, w_ada, b_ada, g_norm1, w_in, pool_w, pool_scale, gdn_conv, gdn_a_log, gdn_dt_bias, gdn_norm_g, na_rpb, w_branch_pool, w_branch_gdn, w_branch_na, w_out, g_norm2, w_up, ffn_conv, w_down, g_final):
    x = jnp.concatenate([x_prompt.reshape(N_CTX, D_MODEL), x_sample.reshape(N_LAT, D_MODEL)], axis=0)

    cvec = jnp.zeros((8, D_MODEL), f32).at[0].set(c_ctx).at[1:1 + DEC_BATCH].set(c)
    mods = _modulation(cvec, w_ada, b_ada)
    group_row = np.array([0] * (N_CTX // ROW_GROUP) + list(range(1, 1 + DEC_BATCH)))
    mods = mods[:, group_row].reshape(DEPTH, N_GROUPS, 6, 1, D_MODEL).transpose(0, 2, 1, 3, 4)

    zero_state = jnp.zeros((BATCH, 2, GDN_HEADS, GDN_HEAD_DIM, GDN_HEAD_DIM), f32)
    new_k, new_v, new_s = [], [], []
    for l in range(DEPTH):
        sh1, sc1, g1, sh2, sc2, g2 = (mods[l, j] for j in range(6))
        w_l = w_in[l]
        o_b = A_W
        o_c = o_b + GATE_LANES
        o_g = o_c + C_W
        w_a = w_l[:, :A_W].astype(bf16)
        w_b = jnp.pad(w_l[:, o_b:o_c], ((0, 0), (0, B_W - GATE_LANES))).astype(bf16)
        w_c = w_l[:, o_c:o_g].astype(bf16)
        w_g = w_l[:, o_g:].astype(bf16)

        h = _norm_mod(x, g_norm1[l], sc1, sh1)
        u_a = _matmul(h, w_a, bf16, "in_proj_mix")
        u_b = _matmul(h, w_b, f32, "in_proj_gates")
        u_c = _matmul(h, w_c, f32, "in_proj_attn")
        u_g = _matmul(h, w_g, bf16, "in_proj_branch_gates")

        y_pool = _pool(u_a, pool_w[l].astype(bf16), pool_scale[l])

        gcols, grows = _gate_prep(u_b, gdn_a_log[l], gdn_dt_bias[l])
        y_gdn_ctx, s_ctx = _gdn(u_a, gcols, grows, gdn_conv[l], gdn_norm_g[l], zero_state, SEQ, BATCH, 0)
        y_gdn_lat, _ = _gdn(u_a, gcols, grows, gdn_conv[l], gdn_norm_g[l], state_gdn[:, l], DEC_SEQ, DEC_BATCH, N_CTX)
        y_gdn = jnp.concatenate([y_gdn_ctx, y_gdn_lat], axis=0)

        y_na_ctx = _ctx_attention(u_c)
        y_na_lat = _na_attention(u_c, cache_na_k[:, l].reshape(DEC_BATCH, PAST_LEN, NA_W),
                                 cache_na_v[:, l].reshape(DEC_BATCH, PAST_LEN, NA_W), _na_bias_tiles(na_rpb[l]))
        y_na = jnp.concatenate([y_na_ctx, y_na_lat], axis=0)

        merged = _merge(y_pool, y_gdn, y_na, w_branch_pool[l].astype(bf16), w_branch_gdn[l].astype(bf16),
                        w_branch_na[l].astype(bf16), u_g)
        x = _matmul_residual(merged, w_out[l].astype(bf16), x, g1, "out_proj")

        h = _norm_mod(x, g_norm2[l], sc2, sh2)
        up = _matmul(h, w_up[l].astype(bf16), bf16, "ffn_up")
        act = _conv_glu(up, ffn_conv[l])
        x = _matmul_residual(act, w_down[l].astype(bf16), x, g2, "ffn_down")

        new_k.append(u_c[:N_CTX, NA_W:2 * NA_W].reshape(BATCH, SEQ, NA_HEADS, NA_HEAD_DIM))
        new_v.append(u_c[:N_CTX, 2 * NA_W:].reshape(BATCH, SEQ, NA_HEADS, NA_HEAD_DIM))
        new_s.append(s_ctx)

    y_prompt = _final_norm(x, g_final, N_CTX, 0).reshape(BATCH, SEQ, D_MODEL)
    y_sample = _final_norm(x, g_final, N_LAT, N_CTX).reshape(DEC_BATCH, DEC_SEQ, D_MODEL)
    return (y_prompt, y_sample, jnp.stack(new_k, axis=1), jnp.stack(new_v, axis=1), jnp.stack(new_s, axis=1))
```

```python
import functools
import math

import numpy as np
import jax
import jax.numpy as jnp
from jax import lax
from jax.experimental import pallas as pl
from jax.experimental.pallas import tpu as pltpu

D_MODEL = 2048
BATCH = 16
SEQ = 256
DEPTH = 2
DEC_BATCH = 2
DEC_SEQ = 2048
PAST_LEN = 512
GRID_W = 64
POOL_GROUPS = 4
POOL_GC = 128
POOL_W = POOL_GROUPS * POOL_GC
POOL_WINDOWS = (2, 4, 8, 16)
GDN_HEADS = 8
GDN_HEAD_DIM = 128
GDN_W = GDN_HEADS * GDN_HEAD_DIM
GDN_CHUNK = 64
NA_HEADS = 8
NA_HEAD_DIM = 64
NA_W = NA_HEADS * NA_HEAD_DIM
NA_WR = 8
NA_WC = 16
D_FF = 5632
EPS = 1e-6

N_CTX = BATCH * SEQ
N_LAT = DEC_BATCH * DEC_SEQ
N_TOK = N_CTX + N_LAT
ROW_GROUP = DEC_SEQ
N_GROUPS = N_TOK // ROW_GROUP
SEQ_TILE = SEQ
HALO = 16
GRID_ROWS = DEC_SEQ // GRID_W
NEG_BIG = -1e30

A_W = POOL_W + 4 * GDN_W
B_W = 128
C_W = 3 * NA_W
G_W = 3 * D_MODEL

VMEM_LIMIT = 48 * 1024 * 1024

f32 = jnp.float32
bf16 = jnp.bfloat16


def _cparams(n_axes):
    return pltpu.CompilerParams(dimension_semantics=("arbitrary",) * n_axes, vmem_limit_bytes=VMEM_LIMIT)


def _dot(a, b):
    return jnp.dot(a.astype(bf16), b.astype(bf16), preferred_element_type=f32)


def _dot_nt(a, b):
    return lax.dot_general(a.astype(bf16), b.astype(bf16), (((1,), (1,)), ((), ())), preferred_element_type=f32)


def _dot_tn(a, b):
    return lax.dot_general(a.astype(bf16), b.astype(bf16), (((0,), (0,)), ((), ())), preferred_element_type=f32)


def _silu(x):
    return x * jax.nn.sigmoid(x)


def _col_tile(n, cap=1536):
    best = 128
    for t in range(128, cap + 1, 128):
        if n % t == 0:
            best = t
    return best


def _mod_kernel(c_ref, w_ref, b_ref, o_ref):
    a = _silu(c_ref[...])
    o_ref[...] = _dot(a, w_ref[...]) + b_ref[...]


def _modulation(cvec, w_ada, b_ada):
    tn = 1024
    n = 6 * D_MODEL
    return pl.pallas_call(
        _mod_kernel,
        grid=(DEPTH, n // tn),
        in_specs=[
            pl.BlockSpec((8, D_MODEL), lambda l, j: (0, 0)),
            pl.BlockSpec((None, D_MODEL, tn), lambda l, j: (l, 0, j)),
            pl.BlockSpec((None, 1, tn), lambda l, j: (l, 0, j)),
        ],
        out_specs=pl.BlockSpec((None, 8, tn), lambda l, j: (l, 0, j)),
        out_shape=jax.ShapeDtypeStruct((DEPTH, 8, n), f32),
        compiler_params=_cparams(2),
        name="modulation",
    )(cvec, w_ada, b_ada.reshape(DEPTH, 1, n))


def _mod_spec(tm, tn, col_of):
    return pl.BlockSpec((None, 1, tn), lambda m, n: ((m * tm) // ROW_GROUP, 0, col_of(n)))


def _norm_mod_kernel(x_ref, g_ref, sc_ref, sh_ref, o_ref):
    x = x_ref[...]
    y = x * lax.rsqrt(jnp.mean(x * x, axis=-1, keepdims=True) + EPS) * g_ref[...]
    o_ref[...] = (y * (1.0 + sc_ref[...]) + sh_ref[...]).astype(o_ref.dtype)


def _norm_mod(x, g, scale, shift):
    tm = 512
    mod = pl.BlockSpec((None, 1, D_MODEL), lambda m: ((m * tm) // ROW_GROUP, 0, 0))
    return pl.pallas_call(
        _norm_mod_kernel,
        grid=(N_TOK // tm,),
        in_specs=[
            pl.BlockSpec((tm, D_MODEL), lambda m: (m, 0)),
            pl.BlockSpec((1, D_MODEL), lambda m: (0, 0)),
            mod, mod,
        ],
        out_specs=pl.BlockSpec((tm, D_MODEL), lambda m: (m, 0)),
        out_shape=jax.ShapeDtypeStruct((N_TOK, D_MODEL), bf16),
        compiler_params=_cparams(1),
        name="norm_mod",
    )(x, g.reshape(1, D_MODEL), scale, shift)


def _mm_kernel(x_ref, w_ref, o_ref):
    o_ref[...] = jnp.dot(x_ref[...], w_ref[...], preferred_element_type=f32).astype(o_ref.dtype)


def _matmul(x, w, out_dtype, name):
    m, k = x.shape
    n = w.shape[1]
    tm = 1024
    tn = _col_tile(n, 1024)
    return pl.pallas_call(
        _mm_kernel,
        grid=(m // tm, n // tn),
        in_specs=[
            pl.BlockSpec((tm, k), lambda i, j: (i, 0)),
            pl.BlockSpec((k, tn), lambda i, j: (0, j)),
        ],
        out_specs=pl.BlockSpec((tm, tn), lambda i, j: (i, j)),
        out_shape=jax.ShapeDtypeStruct((m, n), out_dtype),
        compiler_params=_cparams(2),
        name=name,
    )(x, w)


def _mm_res_kernel(a_ref, w_ref, x_ref, g_ref, o_ref):
    y = jnp.dot(a_ref[...], w_ref[...], preferred_element_type=f32)
    o_ref[...] = x_ref[...] + g_ref[...] * y


def _matmul_residual(a, w, x, gate, name):
    m, k = a.shape
    n = w.shape[1]
    tm = 1024
    tn = 1024 if k <= D_MODEL else 512
    return pl.pallas_call(
        _mm_res_kernel,
        grid=(m // tm, n // tn),
        in_specs=[
            pl.BlockSpec((tm, k), lambda i, j: (i, 0)),
            pl.BlockSpec((k, tn), lambda i, j: (0, j)),
            pl.BlockSpec((tm, tn), lambda i, j: (i, j)),
            _mod_spec(tm, tn, lambda j: j),
        ],
        out_specs=pl.BlockSpec((tm, tn), lambda i, j: (i, j)),
        out_shape=jax.ShapeDtypeStruct((m, n), f32),
        compiler_params=_cparams(2),
        name=name,
    )(a, w, x, gate)


MERGE_TM = 1024


def _merge_kernel(yp_ref, ygc_ref, ygl_ref, ync_ref, ynl_ref, wp_ref, wg_ref, wn_ref, g0_ref, g1_ref, g2_ref, o_ref):
    is_ctx = pl.program_id(0) < N_CTX // MERGE_TM

    def branch(y, w_ref, g_ref):
        return jax.nn.sigmoid(g_ref[...].astype(f32)) * jnp.dot(y, w_ref[...], preferred_element_type=f32)

    y_gdn = jnp.where(is_ctx, ygc_ref[...], ygl_ref[...])
    y_na = jnp.where(is_ctx, ync_ref[...], ynl_ref[...])
    acc = branch(yp_ref[...], wp_ref, g0_ref) + branch(y_gdn, wg_ref, g1_ref) + branch(y_na, wn_ref, g2_ref)
    o_ref[...] = acc.astype(o_ref.dtype)


def _merge(y_pool, y_gdn_ctx, y_gdn_lat, y_na_ctx, y_na_lat, w_pool, w_gdn, w_na, u_gate):
    tm = MERGE_TM
    tn = 1024
    nb = D_MODEL // tn
    n_ctx = N_CTX // tm
    row = lambda width: pl.BlockSpec((tm, width), lambda i, j: (i, 0))
    ctx = lambda width: pl.BlockSpec((tm, width), lambda i, j: (jnp.minimum(i, n_ctx - 1), 0))
    lat = lambda width: pl.BlockSpec((tm, width), lambda i, j: (jnp.maximum(i - n_ctx, 0), 0))
    wcol = lambda kdim: pl.BlockSpec((kdim, tn), lambda i, j: (0, j))
    gate = lambda b: pl.BlockSpec((tm, tn), lambda i, j: (i, j + b * nb))
    return pl.pallas_call(
        _merge_kernel,
        grid=(N_TOK // tm, nb),
        in_specs=[row(POOL_W), ctx(GDN_W), lat(GDN_W), ctx(NA_W), lat(NA_W),
                  wcol(POOL_W), wcol(GDN_W), wcol(NA_W), gate(0), gate(1), gate(2)],
        out_specs=pl.BlockSpec((tm, tn), lambda i, j: (i, j)),
        out_shape=jax.ShapeDtypeStruct((N_TOK, D_MODEL), bf16),
        compiler_params=_cparams(2),
        name="merge",
    )(y_pool, y_gdn_ctx, y_gdn_lat, y_na_ctx, y_na_lat, w_pool, w_gdn, w_na, u_gate, u_gate, u_gate)


def _tile_seq_len(m):
    return jnp.where(m * SEQ_TILE < N_CTX, SEQ, DEC_SEQ)


def _halo_specs(width, col_of):
    per = SEQ_TILE // HALO
    last = N_TOK // HALO - 1
    main = pl.BlockSpec((SEQ_TILE, width), lambda m, c: (m, col_of(c)))
    prev = pl.BlockSpec((HALO, width), lambda m, c: (jnp.maximum(m * per - 1, 0), col_of(c)))
    nxt = pl.BlockSpec((HALO, width), lambda m, c: (jnp.minimum((m + 1) * per, last), col_of(c)))
    return main, prev, nxt


def _tile_pos(m):
    t_len = _tile_seq_len(m)
    row = lax.broadcasted_iota(jnp.int32, (SEQ_TILE, 1), 0)
    return (m * SEQ_TILE + row) & (t_len - 1), t_len


def _pool_kernel(x_ref, xp_ref, xn_ref, w_ref, s_ref, o_ref, ext_ref):
    m = pl.program_id(0)
    pos, t_len = _tile_pos(m)
    first = ((m * SEQ_TILE) & (t_len - 1)) == 0
    last = (((m + 1) * SEQ_TILE) & (t_len - 1)) == 0
    x = x_ref[...].astype(f32)
    ext_ref[0:HALO, :] = jnp.where(first, 0.0, xp_ref[...].astype(f32))
    ext_ref[HALO:HALO + SEQ_TILE, :] = x
    ext_ref[HALO + SEQ_TILE:, :] = jnp.where(last, 0.0, xn_ref[...].astype(f32))
    outs = []
    for g, win in enumerate(POOL_WINDOWS):
        cols = slice(g * POOL_GC, (g + 1) * POOL_GC)
        acc = jnp.zeros((SEQ_TILE, POOL_GC), f32)
        for s in range(-(win // 2), win - win // 2):
            acc = acc + ext_ref[HALO + s:HALO + s + SEQ_TILE, cols]
        lo = jnp.maximum(pos - win // 2, 0)
        hi = jnp.minimum(pos + win - 1 - win // 2, t_len - 1)
        mean = acc / (hi - lo + 1).astype(f32)
        y = _dot(mean - x[:, cols], w_ref[g]) * s_ref[:, cols]
        outs.append(y)
    o_ref[...] = jnp.concatenate(outs, axis=1).astype(o_ref.dtype)


def _pool(u_a, pool_w, pool_scale):
    main, prev, nxt = _halo_specs(POOL_W, lambda c: 0)
    return pl.pallas_call(
        _pool_kernel,
        grid=(N_TOK // SEQ_TILE, 1),
        in_specs=[main, prev, nxt,
                  pl.BlockSpec((POOL_GROUPS, POOL_GC, POOL_GC), lambda m, c: (0, 0, 0)),
                  pl.BlockSpec((1, POOL_W), lambda m, c: (0, 0))],
        out_specs=pl.BlockSpec((SEQ_TILE, POOL_W), lambda m, c: (m, 0)),
        out_shape=jax.ShapeDtypeStruct((N_TOK, POOL_W), bf16),
        scratch_shapes=[pltpu.VMEM((SEQ_TILE + 2 * HALO, POOL_W), f32)],
        compiler_params=_cparams(2),
        name="pool",
    )(u_a, u_a, u_a, pool_w, pool_scale.reshape(1, POOL_W))


def _conv3_tile(x_ref, xp_ref, xn_ref, w_ref, pos, t_len):
    x = x_ref[...].astype(f32)
    row = lax.broadcasted_iota(jnp.int32, (SEQ_TILE, 1), 0)
    x_prev = xp_ref[...].astype(f32)[HALO - 1:HALO, :]
    x_next = xn_ref[...].astype(f32)[0:1, :]
    before = jnp.where(row == 0, x_prev, pltpu.roll(x, 1, 0))
    after = jnp.where(row == SEQ_TILE - 1, x_next, pltpu.roll(x, SEQ_TILE - 1, 0))
    before = jnp.where(pos == 0, 0.0, before)
    after = jnp.where(pos == t_len - 1, 0.0, after)
    return before * w_ref[0:1, :] + x * w_ref[1:2, :] + after * w_ref[2:3, :]


def _conv_glu_kernel(a_ref, ap_ref, an_ref, b_ref, bp_ref, bn_ref, wa_ref, wb_ref, o_ref):
    pos, t_len = _tile_pos(pl.program_id(0))
    a = _conv3_tile(a_ref, ap_ref, an_ref, wa_ref, pos, t_len)
    b = _conv3_tile(b_ref, bp_ref, bn_ref, wb_ref, pos, t_len)
    o_ref[...] = (_silu(a) * b).astype(o_ref.dtype)


def _conv_glu(up, ffn_conv):
    tc = 512
    nb = D_FF // tc
    a_specs = _halo_specs(tc, lambda c: c)
    b_specs = _halo_specs(tc, lambda c: c + nb)
    return pl.pallas_call(
        _conv_glu_kernel,
        grid=(N_TOK // SEQ_TILE, nb),
        in_specs=[*a_specs, *b_specs,
                  pl.BlockSpec((3, tc), lambda m, c: (0, c)),
                  pl.BlockSpec((3, tc), lambda m, c: (0, c + nb))],
        out_specs=pl.BlockSpec((SEQ_TILE, tc), lambda m, c: (m, c)),
        out_shape=jax.ShapeDtypeStruct((N_TOK, D_FF), bf16),
        compiler_params=_cparams(2),
        name="conv_glu",
    )(up, up, up, up, up, up, ffn_conv, ffn_conv)


GATE_LANES = 4 * GDN_HEADS
CHUNKS_PER_TILE = SEQ_TILE // GDN_CHUNK


def _gate_prep_kernel(u_ref, alog_ref, dtb_ref, cols_ref, rows_ref):
    u = u_ref[...]
    lane = lax.broadcasted_iota(jnp.int32, (1, B_W), 1)
    x = u + dtb_ref[...]
    softplus = jnp.maximum(x, 0.0) + jnp.log1p(jnp.exp(-jnp.abs(x)))
    g = -jnp.exp(alog_ref[...]) * softplus
    r = lax.broadcasted_iota(jnp.int32, (SEQ_TILE, SEQ_TILE), 0)
    c = lax.broadcasted_iota(jnp.int32, (SEQ_TILE, SEQ_TILE), 1)
    shift = int(math.log2(GDN_CHUNK))
    same = (r >> shift) == (c >> shift)
    tri_f = jnp.where(same & (c <= r), 1.0, 0.0)
    tri_b = jnp.where(same & (c >= r), 1.0, 0.0)
    gc_f = jnp.dot(tri_f, g, preferred_element_type=f32, precision=lax.Precision.HIGHEST)
    gc_b = jnp.dot(tri_b, g, preferred_element_type=f32, precision=lax.Precision.HIGHEST)
    cols = jnp.where(lane < 2 * GDN_HEADS, jax.nn.sigmoid(u), jnp.where(lane < 3 * GDN_HEADS, gc_f, gc_b))
    cols_ref[...] = cols
    for j in range(CHUNKS_PER_TILE):
        t = cols[j * GDN_CHUNK:(j + 1) * GDN_CHUNK, :].T
        rows_ref[j] = t[0:GATE_LANES, :]


def _gate_prep(u_b, a_log, dt_bias):
    pad = lambda p: jnp.zeros((1, B_W), f32).at[0, 2 * GDN_HEADS:GATE_LANES].set(p.reshape(-1))
    n_tiles = N_TOK // SEQ_TILE
    return pl.pallas_call(
        _gate_prep_kernel,
        grid=(n_tiles,),
        in_specs=[pl.BlockSpec((SEQ_TILE, B_W), lambda m: (m, 0)),
                  pl.BlockSpec((1, B_W), lambda m: (0, 0)),
                  pl.BlockSpec((1, B_W), lambda m: (0, 0))],
        out_specs=[pl.BlockSpec((SEQ_TILE, B_W), lambda m: (m, 0)),
                   pl.BlockSpec((CHUNKS_PER_TILE, GATE_LANES, GDN_CHUNK), lambda m: (m, 0, 0))],
        out_shape=[jax.ShapeDtypeStruct((N_TOK, B_W), f32),
                   jax.ShapeDtypeStruct((N_TOK // GDN_CHUNK, GATE_LANES, GDN_CHUNK), f32)],
        compiler_params=_cparams(1),
        name="gdn_gate_prep",
    )(u_b, pad(a_log), pad(dt_bias))


def _gdn_conv_silu(x_ref, w_ref, t_len):
    x = x_ref[...].astype(f32)
    row = lax.broadcasted_iota(jnp.int32, (t_len, 1), 0)
    before = jnp.where(row == 0, 0.0, pltpu.roll(x, 1, 0))
    after = jnp.where(row == t_len - 1, 0.0, pltpu.roll(x, t_len - 1, 0))
    return _silu(before * w_ref[0:1, :] + x * w_ref[1:2, :] + after * w_ref[2:3, :])


def _l2norm(x):
    return x * lax.rsqrt(jnp.sum(x * x, axis=-1, keepdims=True) + EPS)


GDN_GROUP = 4
GDN_STEP_ROWS = GDN_HEAD_DIM + GDN_CHUNK


def _gdn_prepare(grp, head, q_s, k_s, v_s, gcol_ref, grow_ref, pq_s, b_s, gl_s, o_s):
    cs, hd = GDN_CHUNK, GDN_HEAD_DIM
    rows = GDN_GROUP * cs
    r0 = pl.multiple_of(grp * rows, rows)
    lane = lax.broadcasted_iota(jnp.int32, (1, B_W), 1)
    ri = lax.broadcasted_iota(jnp.int32, (cs, cs), 0)
    ci = lax.broadcasted_iota(jnp.int32, (cs, cs), 1)
    blk = ri ^ ci
    probs = []
    for j in range(GDN_GROUP):
        rj = r0 + j * cs
        q = q_s[pl.ds(rj, cs), :]
        k = k_s[pl.ds(rj, cs), :]
        v = v_s[pl.ds(rj, cs), :]
        gates = gcol_ref[pl.ds(rj, cs), :]
        pick = lambda idx, gates=gates: jnp.sum(jnp.where(lane == idx, gates, 0.0), axis=1, keepdims=True)
        kk = _dot_nt(k, k)
        qk = _dot_nt(q, k)
        for direction in range(2):
            beta = pick(direction * GDN_HEADS + head)
            dec_lane = (2 + direction) * GDN_HEADS + head
            gc = pick(dec_lane)
            gr = grow_ref[grp * GDN_GROUP + j, pl.ds(dec_lane, 1), :]
            if direction == 0:
                incl, strict = ri >= ci, ri > ci
                g_tot = gc[cs - 1:cs, :]
            else:
                incl, strict = ri <= ci, ri < ci
                g_tot = gc[0:1, :]
            decay = jnp.exp(jnp.where(incl, gc - gr, -jnp.inf))
            probs.append(dict(j=j, d=direction, q=q, k=k, v=v, beta=beta, gc=gc, g_tot=g_tot,
                              qk=qk * decay, lmat=jnp.where(strict, kk * decay * beta, 0.0)))
    for p in probs:
        p["n"] = -jnp.where(blk < 2, p["lmat"], 0.0)
    for level in range(1, int(math.log2(cs))):
        mask = (blk >= (1 << level)) & (blk < (2 << level))
        for p in probs:
            c_k = jnp.where(mask, p["lmat"], 0.0)
            y = c_k + _dot(c_k, p["n"])
            p["n"] = p["n"] - y - _dot(p["n"], y)
    for p in probs:
        j, d = p["j"], p["d"]
        e_gc = jnp.exp(p["gc"])
        vb = p["v"] * p["beta"]
        kbg = p["k"] * (p["beta"] * e_gc)
        uw = _dot(p["n"], jnp.concatenate([vb, kbg], axis=1))
        wu = jnp.concatenate([kbg + uw[:, hd:], vb + uw[:, 0:hd]], axis=1)
        kd = p["k"] * jnp.exp(p["g_tot"] - p["gc"])
        top = _dot_tn(kd, wu)
        bot = _dot(p["qk"], wu)
        c = grp * GDN_GROUP + j
        pq_s[d, c, 0:hd, :] = top[:, 0:hd].astype(pq_s.dtype)
        pq_s[d, c, hd:, :] = (p["q"] * e_gc - bot[:, 0:hd]).astype(pq_s.dtype)
        b_s[d, c] = top[:, hd:]
        o_s[d, pl.ds(r0 + j * cs, cs), :] = bot[:, hd:]
        gl_s[d, c] = jnp.broadcast_to(jnp.exp(p["g_tot"]), (1, hd))


def _gdn_scan_step(c, d, state, pq_s, b_s, gl_s, o_s):
    hd = GDN_HEAD_DIM
    r0 = pl.multiple_of(c * GDN_CHUNK, GDN_CHUNK)
    r = _dot(pq_s[d, c], state)
    o_s[d, pl.ds(r0, GDN_CHUNK), :] += r[hd:]
    return state * gl_s[d, c] - r[0:hd] + b_s[d, c]


def _gdn_kernel(q_ref, k_ref, v_ref, z_ref, wq_ref, wk_ref, wv_ref, gcol_ref, grow_ref, ng_ref, s0_ref,
                y_ref, sfin_ref, q_s, k_s, v_s, o_s, pq_s, b_s, gl_s, *, t_len):
    head = pl.program_id(1)
    n_chunks = t_len // GDN_CHUNK
    q_s[...] = _l2norm(_gdn_conv_silu(q_ref, wq_ref, t_len)) * (GDN_HEAD_DIM ** -0.5)
    k_s[...] = _l2norm(_gdn_conv_silu(k_ref, wk_ref, t_len))
    v_s[...] = _gdn_conv_silu(v_ref, wv_ref, t_len)

    def prepare(grp, carry):
        _gdn_prepare(grp, head, q_s, k_s, v_s, gcol_ref, grow_ref, pq_s, b_s, gl_s, o_s)
        return carry

    lax.fori_loop(0, n_chunks // GDN_GROUP, prepare, 0)

    def scan(i, states):
        s_f = _gdn_scan_step(i, 0, states[0], pq_s, b_s, gl_s, o_s)
        s_b = _gdn_scan_step(n_chunks - 1 - i, 1, states[1], pq_s, b_s, gl_s, o_s)
        return s_f, s_b

    s_f, s_b = lax.fori_loop(0, n_chunks, scan, (s0_ref[0], s0_ref[1]))
    sfin_ref[0] = s_f
    sfin_ref[1] = s_b
    o = o_s[0] + o_s[1]
    o = o * lax.rsqrt(jnp.mean(o * o, axis=-1, keepdims=True) + EPS) * ng_ref[...]
    y_ref[...] = (o * _silu(z_ref[...].astype(f32))).astype(y_ref.dtype)


def _gdn(u_a, gcols, grows, conv_w, norm_g, s0, t_len, n_seq, row_off):
    hd = GDN_HEAD_DIM
    nh = GDN_HEADS
    blk0 = row_off // t_len
    c_q = POOL_W // hd
    col = lambda base: pl.BlockSpec((t_len, hd), lambda s, h: (blk0 + s, base + h))
    cw = lambda base: pl.BlockSpec((3, hd), lambda s, h: (0, base + h))
    n_chunks = t_len // GDN_CHUNK
    state_spec = pl.BlockSpec((None, 2, None, hd, hd), lambda s, h: (s, 0, h, 0, 0))
    return pl.pallas_call(
        functools.partial(_gdn_kernel, t_len=t_len),
        grid=(n_seq, nh),
        in_specs=[col(c_q), col(c_q + nh), col(c_q + 2 * nh), col(c_q + 3 * nh),
                  cw(0), cw(nh), cw(2 * nh),
                  pl.BlockSpec((t_len, B_W), lambda s, h: (blk0 + s, 0)),
                  pl.BlockSpec((n_chunks, GATE_LANES, GDN_CHUNK), lambda s, h: (blk0 + s, 0, 0)),
                  pl.BlockSpec((1, hd), lambda s, h: (0, 0)),
                  state_spec],
        out_specs=[pl.BlockSpec((t_len, hd), lambda s, h: (s, h)), state_spec],
        out_shape=[jax.ShapeDtypeStruct((n_seq * t_len, GDN_W), bf16),
                   jax.ShapeDtypeStruct((n_seq, 2, nh, hd, hd), f32)],
        scratch_shapes=[pltpu.VMEM((t_len, hd), f32)] * 3
        + [pltpu.VMEM((2, t_len, hd), f32),
           pltpu.VMEM((2, n_chunks, GDN_STEP_ROWS, hd), bf16),
           pltpu.VMEM((2, n_chunks, hd, hd), f32),
           pltpu.VMEM((2, n_chunks, 1, hd), f32)],
        compiler_params=_cparams(2),
        name=f"gdn_t{t_len}",
    )(u_a, u_a, u_a, u_a, conv_w, conv_w, conv_w, gcols, grows, norm_g.reshape(1, hd), s0)


def _ctx_attn_kernel(x_ref, o_ref):
    hd = NA_HEAD_DIM
    outs = []
    for h in range(NA_HEADS):
        q = x_ref[:, h * hd:(h + 1) * hd]
        k = x_ref[:, NA_W + h * hd:NA_W + (h + 1) * hd]
        v = x_ref[:, 2 * NA_W + h * hd:2 * NA_W + (h + 1) * hd]
        s = _dot_nt(q, k) * (hd ** -0.5)
        p = jnp.exp(s - jnp.max(s, axis=-1, keepdims=True))
        outs.append(_dot(p, v) / jnp.sum(p, axis=-1, keepdims=True))
    o_ref[...] = jnp.concatenate(outs, axis=1).astype(o_ref.dtype)


def _ctx_attention(u_c):
    return pl.pallas_call(
        _ctx_attn_kernel,
        grid=(BATCH,),
        in_specs=[pl.BlockSpec((SEQ, C_W), lambda b: (b, 0))],
        out_specs=pl.BlockSpec((SEQ, NA_W), lambda b: (b, 0)),
        out_shape=jax.ShapeDtypeStruct((N_CTX, NA_W), bf16),
        compiler_params=_cparams(1),
        name="ctx_attention",
    )(u_c)


HEADS_PER_STEP = 128 // NA_HEAD_DIM
NB_KEYS = NA_WR * GRID_W


def _na_kernel(q_ref, k_ref, v_ref, kc_ref, vc_ref, bias_ref, o_ref):
    hd = NA_HEAD_DIM
    scale = hd ** -0.5

    def body(r, carry):
        start = jnp.clip(r - NA_WR // 2, 0, GRID_ROWS - NA_WR)
        case = start - r + NA_WR - 1
        q0 = pl.multiple_of(r * GRID_W, GRID_W)
        k0 = pl.multiple_of(start * GRID_W, GRID_W)
        outs = []
        for hh in range(HEADS_PER_STEP):
            cols = slice(hh * hd, (hh + 1) * hd)
            q = q_ref[pl.ds(q0, GRID_W), cols]
            s_nb = _dot_nt(q, k_ref[pl.ds(k0, NB_KEYS), cols]) * scale + bias_ref[hh, case]
            s_cx = _dot_nt(q, kc_ref[:, cols]) * scale
            mx = jnp.maximum(jnp.max(s_nb, axis=-1, keepdims=True), jnp.max(s_cx, axis=-1, keepdims=True))
            p_nb = jnp.exp(s_nb - mx)
            p_cx = jnp.exp(s_cx - mx)
            den = jnp.sum(p_nb, axis=-1, keepdims=True) + jnp.sum(p_cx, axis=-1, keepdims=True)
            o = _dot(p_nb, v_ref[pl.ds(k0, NB_KEYS), cols]) + _dot(p_cx, vc_ref[:, cols])
            outs.append(o / den)
        o_ref[pl.ds(q0, GRID_W), :] = jnp.concatenate(outs, axis=1).astype(o_ref.dtype)
        return carry

    lax.fori_loop(0, GRID_ROWS, body, 0)


def _na_bias_tiles(rpb):
    col = np.arange(GRID_W)
    col_start = np.clip(col - NA_WC // 2, 0, GRID_W - NA_WC)
    col_mask = (col[None, :] >= col_start[:, None]) & (col[None, :] < col_start[:, None] + NA_WC)
    d_col = np.clip(col[None, :] - col[:, None], -(NA_WC - 1), NA_WC - 1) + NA_WC - 1
    table = jnp.where(col_mask[None, None], rpb[:, :, d_col], NEG_BIG)
    tiles = [table[:, case:case + NA_WR].transpose(0, 2, 1, 3).reshape(NA_HEADS, GRID_W, NB_KEYS)
             for case in range(NA_WR)]
    return jnp.stack(tiles, axis=1).astype(f32)


def _na_attention(u_c, k_ctx, v_ctx, bias_tiles):
    blk0 = N_CTX // DEC_SEQ
    nhp = NA_HEADS // HEADS_PER_STEP
    col = lambda base: pl.BlockSpec((DEC_SEQ, 128), lambda b, p: (blk0 + b, base + p))
    cache = pl.BlockSpec((None, PAST_LEN, 128), lambda b, p: (b, 0, p))
    return pl.pallas_call(
        _na_kernel,
        grid=(DEC_BATCH, nhp),
        in_specs=[col(0), col(nhp), col(2 * nhp), cache, cache,
                  pl.BlockSpec((HEADS_PER_STEP, NA_WR, GRID_W, NB_KEYS), lambda b, p: (p, 0, 0, 0))],
        out_specs=pl.BlockSpec((DEC_SEQ, 128), lambda b, p: (b, p)),
        out_shape=jax.ShapeDtypeStruct((N_LAT, NA_W), bf16),
        compiler_params=_cparams(2),
        name="na_attention",
    )(u_c, u_c, u_c, k_ctx, v_ctx, bias_tiles)


def _final_norm_kernel(x_ref, g_ref, o_ref):
    x = x_ref[...]
    o_ref[...] = x * lax.rsqrt(jnp.mean(x * x, axis=-1, keepdims=True) + EPS) * g_ref[...]


def _final_norm(x, g, n_rows, row_off):
    tm = 512
    blk0 = row_off // tm
    return pl.pallas_call(
        _final_norm_kernel,
        grid=(n_rows // tm,),
        in_specs=[pl.BlockSpec((tm, D_MODEL), lambda m: (blk0 + m, 0)),
                  pl.BlockSpec((1, D_MODEL), lambda m: (0, 0))],
        out_specs=pl.BlockSpec((tm, D_MODEL), lambda m: (m, 0)),
        out_shape=jax.ShapeDtypeStruct((n_rows, D_MODEL), f32),
        compiler_params=_cparams(1),
        name="final_norm",
    )(x, g.reshape(1, D_MODEL))


def kernel(x_prompt, x_sample, cache_na_k, cache_na_v, state_gdn, c, c_ctx, w_ada, b_ada, g_norm1, w_in, pool_w, pool_scale, gdn_conv, gdn_a_log, gdn_dt_bias, gdn_norm_g, na_rpb, w_branch_pool, w_branch_gdn, w_branch_na, w_out, g_norm2, w_up, ffn_conv, w_down, g_final):
    x = jnp.concatenate([x_prompt.reshape(N_CTX, D_MODEL), x_sample.reshape(N_LAT, D_MODEL)], axis=0)

    cvec = jnp.zeros((8, D_MODEL), f32).at[0].set(c_ctx).at[1:1 + DEC_BATCH].set(c)
    mods = _modulation(cvec, w_ada, b_ada)
    group_row = np.array([0] * (N_CTX // ROW_GROUP) + list(range(1, 1 + DEC_BATCH)))
    mods = mods[:, group_row].reshape(DEPTH, N_GROUPS, 6, 1, D_MODEL).transpose(0, 2, 1, 3, 4)

    zero_state = jnp.zeros((BATCH, 2, GDN_HEADS, GDN_HEAD_DIM, GDN_HEAD_DIM), f32)
    new_k, new_v, new_s = [], [], []
    for l in range(DEPTH):
        sh1, sc1, g1, sh2, sc2, g2 = (mods[l, j] for j in range(6))
        w_l = w_in[l]
        o_b = A_W
        o_c = o_b + GATE_LANES
        o_g = o_c + C_W
        w_a = w_l[:, :A_W].astype(bf16)
        w_b = jnp.pad(w_l[:, o_b:o_c], ((0, 0), (0, B_W - GATE_LANES))).astype(bf16)
        w_c = w_l[:, o_c:o_g].astype(bf16)
        w_g = w_l[:, o_g:].astype(bf16)

        h = _norm_mod(x, g_norm1[l], sc1, sh1)
        u_a = _matmul(h, w_a, bf16, "in_proj_mix")
        u_b = _matmul(h, w_b, f32, "in_proj_gates")
        u_c = _matmul(h, w_c, f32, "in_proj_attn")
        u_g = _matmul(h, w_g, bf16, "in_proj_branch_gates")

        y_pool = _pool(u_a, pool_w[l].astype(bf16), pool_scale[l])

        gcols, grows = _gate_prep(u_b, gdn_a_log[l], gdn_dt_bias[l])
        y_gdn_ctx, s_ctx = _gdn(u_a, gcols, grows, gdn_conv[l], gdn_norm_g[l], zero_state, SEQ, BATCH, 0)
        y_gdn_lat, _ = _gdn(u_a, gcols, grows, gdn_conv[l], gdn_norm_g[l], state_gdn[:, l], DEC_SEQ, DEC_BATCH, N_CTX)

        y_na_ctx = _ctx_attention(u_c)
        y_na_lat = _na_attention(u_c, cache_na_k[:, l].reshape(DEC_BATCH, PAST_LEN, NA_W),
                                 cache_na_v[:, l].reshape(DEC_BATCH, PAST_LEN, NA_W), _na_bias_tiles(na_rpb[l]))

        merged = _merge(y_pool, y_gdn_ctx, y_gdn_lat, y_na_ctx, y_na_lat, w_branch_pool[l].astype(bf16),
                        w_branch_gdn[l].astype(bf16), w_branch_na[l].astype(bf16), u_g)
        x = _matmul_residual(merged, w_out[l].astype(bf16), x, g1, "out_proj")

        h = _norm_mod(x, g_norm2[l], sc2, sh2)
        up = _matmul(h, w_up[l].astype(bf16), bf16, "ffn_up")
        act = _conv_glu(up, ffn_conv[l])
        x = _matmul_residual(act, w_down[l].astype(bf16), x, g2, "ffn_down")

        new_k.append(u_c[:N_CTX, NA_W:2 * NA_W].reshape(BATCH, SEQ, NA_HEADS, NA_HEAD_DIM))
        new_v.append(u_c[:N_CTX, 2 * NA_W:].reshape(BATCH, SEQ, NA_HEADS, NA_HEAD_DIM))
        new_s.append(s_ctx)

    y_prompt = _final_norm(x, g_final, N_CTX, 0).reshape(BATCH, SEQ, D_MODEL)
    y_sample = _final_norm(x, g_final, N_LAT, N_CTX).reshape(DEC_BATCH, DEC_SEQ, D_MODEL)
    return (y_prompt, y_sample, jnp.stack(new_k, axis=1), jnp.stack(new_v, axis=1), jnp.stack(new_s, axis=1))
```

```python
import functools
import math

import numpy as np
import jax
import jax.numpy as jnp
from jax import lax
from jax.experimental import pallas as pl
from jax.experimental.pallas import tpu as pltpu

D_MODEL = 2048
BATCH = 16
SEQ = 256
DEPTH = 2
DEC_BATCH = 2
DEC_SEQ = 2048
PAST_LEN = 512
GRID_W = 64
POOL_GROUPS = 4
POOL_GC = 128
POOL_W = POOL_GROUPS * POOL_GC
POOL_WINDOWS = (2, 4, 8, 16)
GDN_HEADS = 8
GDN_HEAD_DIM = 128
GDN_W = GDN_HEADS * GDN_HEAD_DIM
GDN_CHUNK = 64
NA_HEADS = 8
NA_HEAD_DIM = 64
NA_W = NA_HEADS * NA_HEAD_DIM
NA_WR = 8
NA_WC = 16
D_FF = 5632
EPS = 1e-6

N_CTX = BATCH * SEQ
N_LAT = DEC_BATCH * DEC_SEQ
N_TOK = N_CTX + N_LAT
ROW_GROUP = DEC_SEQ
N_GROUPS = N_TOK // ROW_GROUP
SEQ_TILE = SEQ
HALO = 16
GRID_ROWS = DEC_SEQ // GRID_W
NEG_BIG = -1e30

A_W = POOL_W + 4 * GDN_W
B_W = 128
C_W = 3 * NA_W
G_W = 3 * D_MODEL

VMEM_LIMIT = 48 * 1024 * 1024

f32 = jnp.float32
bf16 = jnp.bfloat16


def _cparams(n_axes):
    return pltpu.CompilerParams(dimension_semantics=("arbitrary",) * n_axes, vmem_limit_bytes=VMEM_LIMIT)


def _dot(a, b):
    return jnp.dot(a.astype(bf16), b.astype(bf16), preferred_element_type=f32)


def _dot_nt(a, b):
    return lax.dot_general(a.astype(bf16), b.astype(bf16), (((1,), (1,)), ((), ())), preferred_element_type=f32)


def _dot_tn(a, b):
    return lax.dot_general(a.astype(bf16), b.astype(bf16), (((0,), (0,)), ((), ())), preferred_element_type=f32)


def _silu(x):
    return x * jax.nn.sigmoid(x)


def _col_tile(n, cap=1536):
    best = 128
    for t in range(128, cap + 1, 128):
        if n % t == 0:
            best = t
    return best


def _mod_kernel(c_ref, w_ref, b_ref, o_ref):
    a = _silu(c_ref[...])
    o_ref[...] = _dot(a, w_ref[...]) + b_ref[...]


def _modulation(cvec, w_ada, b_ada):
    tn = 1024
    n = 6 * D_MODEL
    return pl.pallas_call(
        _mod_kernel,
        grid=(DEPTH, n // tn),
        in_specs=[
            pl.BlockSpec((8, D_MODEL), lambda l, j: (0, 0)),
            pl.BlockSpec((None, D_MODEL, tn), lambda l, j: (l, 0, j)),
            pl.BlockSpec((None, 1, tn), lambda l, j: (l, 0, j)),
        ],
        out_specs=pl.BlockSpec((None, 8, tn), lambda l, j: (l, 0, j)),
        out_shape=jax.ShapeDtypeStruct((DEPTH, 8, n), f32),
        compiler_params=_cparams(2),
        name="modulation",
    )(cvec, w_ada, b_ada.reshape(DEPTH, 1, n))


def _mod_spec(tm, tn, col_of):
    return pl.BlockSpec((None, 1, tn), lambda m, n: ((m * tm) // ROW_GROUP, 0, col_of(n)))


def _norm_mod_kernel(x_ref, g_ref, sc_ref, sh_ref, o_ref):
    x = x_ref[...]
    y = x * lax.rsqrt(jnp.mean(x * x, axis=-1, keepdims=True) + EPS) * g_ref[...]
    o_ref[...] = (y * (1.0 + sc_ref[...]) + sh_ref[...]).astype(o_ref.dtype)


def _norm_mod(x, g, scale, shift):
    tm = 512
    mod = pl.BlockSpec((None, 1, D_MODEL), lambda m: ((m * tm) // ROW_GROUP, 0, 0))
    return pl.pallas_call(
        _norm_mod_kernel,
        grid=(N_TOK // tm,),
        in_specs=[
            pl.BlockSpec((tm, D_MODEL), lambda m: (m, 0)),
            pl.BlockSpec((1, D_MODEL), lambda m: (0, 0)),
            mod, mod,
        ],
        out_specs=pl.BlockSpec((tm, D_MODEL), lambda m: (m, 0)),
        out_shape=jax.ShapeDtypeStruct((N_TOK, D_MODEL), bf16),
        compiler_params=_cparams(1),
        name="norm_mod",
    )(x, g.reshape(1, D_MODEL), scale, shift)


def _mm_kernel(x_ref, w_ref, o_ref):
    o_ref[...] = jnp.dot(x_ref[...], w_ref[...], preferred_element_type=f32).astype(o_ref.dtype)


def _matmul(x, w, out_dtype, name):
    m, k = x.shape
    n = w.shape[1]
    tm = 1024
    tn = _col_tile(n, 1024)
    return pl.pallas_call(
        _mm_kernel,
        grid=(m // tm, n // tn),
        in_specs=[
            pl.BlockSpec((tm, k), lambda i, j: (i, 0)),
            pl.BlockSpec((k, tn), lambda i, j: (0, j)),
        ],
        out_specs=pl.BlockSpec((tm, tn), lambda i, j: (i, j)),
        out_shape=jax.ShapeDtypeStruct((m, n), out_dtype),
        compiler_params=_cparams(2),
        name=name,
    )(x, w)


def _mm_res_kernel(a_ref, w_ref, x_ref, g_ref, o_ref):
    y = jnp.dot(a_ref[...], w_ref[...], preferred_element_type=f32)
    o_ref[...] = x_ref[...] + g_ref[...] * y


def _matmul_residual(a, w, x, gate, name):
    m, k = a.shape
    n = w.shape[1]
    tm = 1024
    tn = 1024 if k <= D_MODEL else 512
    return pl.pallas_call(
        _mm_res_kernel,
        grid=(m // tm, n // tn),
        in_specs=[
            pl.BlockSpec((tm, k), lambda i, j: (i, 0)),
            pl.BlockSpec((k, tn), lambda i, j: (0, j)),
            pl.BlockSpec((tm, tn), lambda i, j: (i, j)),
            _mod_spec(tm, tn, lambda j: j),
        ],
        out_specs=pl.BlockSpec((tm, tn), lambda i, j: (i, j)),
        out_shape=jax.ShapeDtypeStruct((m, n), f32),
        compiler_params=_cparams(2),
        name=name,
    )(a, w, x, gate)


MERGE_TM = 1024


def _merge_kernel(yp_ref, ygc_ref, ygl_ref, ync_ref, ynl_ref, wp_ref, wg_ref, wn_ref, g0_ref, g1_ref, g2_ref, o_ref):
    is_ctx = pl.program_id(0) < N_CTX // MERGE_TM

    def branch(y, w_ref, g_ref):
        return jax.nn.sigmoid(g_ref[...].astype(f32)) * jnp.dot(y, w_ref[...], preferred_element_type=f32)

    y_gdn = jnp.where(is_ctx, ygc_ref[...], ygl_ref[...])
    y_na = jnp.where(is_ctx, ync_ref[...], ynl_ref[...])
    acc = branch(yp_ref[...], wp_ref, g0_ref) + branch(y_gdn, wg_ref, g1_ref) + branch(y_na, wn_ref, g2_ref)
    o_ref[...] = acc.astype(o_ref.dtype)


def _merge(y_pool, y_gdn_ctx, y_gdn_lat, y_na_ctx, y_na_lat, w_pool, w_gdn, w_na, u_gate):
    tm = MERGE_TM
    tn = 1024
    nb = D_MODEL // tn
    n_ctx = N_CTX // tm
    row = lambda width: pl.BlockSpec((tm, width), lambda i, j: (i, 0))
    ctx = lambda width: pl.BlockSpec((tm, width), lambda i, j: (jnp.minimum(i, n_ctx - 1), 0))
    lat = lambda width: pl.BlockSpec((tm, width), lambda i, j: (jnp.maximum(i - n_ctx, 0), 0))
    wcol = lambda kdim: pl.BlockSpec((kdim, tn), lambda i, j: (0, j))
    gate = lambda b: pl.BlockSpec((tm, tn), lambda i, j: (i, j + b * nb))
    return pl.pallas_call(
        _merge_kernel,
        grid=(N_TOK // tm, nb),
        in_specs=[row(POOL_W), ctx(GDN_W), lat(GDN_W), ctx(NA_W), lat(NA_W),
                  wcol(POOL_W), wcol(GDN_W), wcol(NA_W), gate(0), gate(1), gate(2)],
        out_specs=pl.BlockSpec((tm, tn), lambda i, j: (i, j)),
        out_shape=jax.ShapeDtypeStruct((N_TOK, D_MODEL), bf16),
        compiler_params=_cparams(2),
        name="merge",
    )(y_pool, y_gdn_ctx, y_gdn_lat, y_na_ctx, y_na_lat, w_pool, w_gdn, w_na, u_gate, u_gate, u_gate)


def _tile_seq_len(m):
    return jnp.where(m * SEQ_TILE < N_CTX, SEQ, DEC_SEQ)


def _halo_specs(width, col_of):
    per = SEQ_TILE // HALO
    last = N_TOK // HALO - 1
    main = pl.BlockSpec((SEQ_TILE, width), lambda m, c: (m, col_of(c)))
    prev = pl.BlockSpec((HALO, width), lambda m, c: (jnp.maximum(m * per - 1, 0), col_of(c)))
    nxt = pl.BlockSpec((HALO, width), lambda m, c: (jnp.minimum((m + 1) * per, last), col_of(c)))
    return main, prev, nxt


def _tile_pos(m):
    t_len = _tile_seq_len(m)
    row = lax.broadcasted_iota(jnp.int32, (SEQ_TILE, 1), 0)
    return (m * SEQ_TILE + row) & (t_len - 1), t_len


def _pool_kernel(x_ref, xp_ref, xn_ref, w_ref, s_ref, o_ref, ext_ref):
    m = pl.program_id(0)
    pos, t_len = _tile_pos(m)
    first = ((m * SEQ_TILE) & (t_len - 1)) == 0
    last = (((m + 1) * SEQ_TILE) & (t_len - 1)) == 0
    x = x_ref[...].astype(f32)
    ext_ref[0:HALO, :] = jnp.where(first, 0.0, xp_ref[...].astype(f32))
    ext_ref[HALO:HALO + SEQ_TILE, :] = x
    ext_ref[HALO + SEQ_TILE:, :] = jnp.where(last, 0.0, xn_ref[...].astype(f32))
    outs = []
    for g, win in enumerate(POOL_WINDOWS):
        cols = slice(g * POOL_GC, (g + 1) * POOL_GC)
        acc = jnp.zeros((SEQ_TILE, POOL_GC), f32)
        for s in range(-(win // 2), win - win // 2):
            acc = acc + ext_ref[HALO + s:HALO + s + SEQ_TILE, cols]
        lo = jnp.maximum(pos - win // 2, 0)
        hi = jnp.minimum(pos + win - 1 - win // 2, t_len - 1)
        mean = acc / (hi - lo + 1).astype(f32)
        y = _dot(mean - x[:, cols], w_ref[g]) * s_ref[:, cols]
        outs.append(y)
    o_ref[...] = jnp.concatenate(outs, axis=1).astype(o_ref.dtype)


def _pool(u_a, pool_w, pool_scale):
    main, prev, nxt = _halo_specs(POOL_W, lambda c: 0)
    return pl.pallas_call(
        _pool_kernel,
        grid=(N_TOK // SEQ_TILE, 1),
        in_specs=[main, prev, nxt,
                  pl.BlockSpec((POOL_GROUPS, POOL_GC, POOL_GC), lambda m, c: (0, 0, 0)),
                  pl.BlockSpec((1, POOL_W), lambda m, c: (0, 0))],
        out_specs=pl.BlockSpec((SEQ_TILE, POOL_W), lambda m, c: (m, 0)),
        out_shape=jax.ShapeDtypeStruct((N_TOK, POOL_W), bf16),
        scratch_shapes=[pltpu.VMEM((SEQ_TILE + 2 * HALO, POOL_W), f32)],
        compiler_params=_cparams(2),
        name="pool",
    )(u_a, u_a, u_a, pool_w, pool_scale.reshape(1, POOL_W))


FFN_TM = 1024
FFN_TC = 512
FFN_SUB = 128


def _ffn_up_kernel(x_ref, xp_ref, xn_ref, wa_ref, wb_ref, ca_ref, cb_ref, o_ref, w_s):
    i = pl.program_id(1)
    n_sub = FFN_TC // FFN_SUB
    sub = FFN_SUB

    @pl.when(i == 0)
    def _():
        for s in range(n_sub):
            w_s[:, 2 * s * sub:(2 * s + 1) * sub] = wa_ref[:, s * sub:(s + 1) * sub].astype(bf16)
            w_s[:, (2 * s + 1) * sub:(2 * s + 2) * sub] = wb_ref[:, s * sub:(s + 1) * sub].astype(bf16)

    rows = FFN_TM + 2 * HALO
    x_ext = jnp.concatenate([xp_ref[...], x_ref[...], xn_ref[...]], axis=0)
    t_len = jnp.where(i * FFN_TM < N_CTX, SEQ, DEC_SEQ)
    pos = (i * FFN_TM + lax.broadcasted_iota(jnp.int32, (FFN_TM, 1), 0)) & (t_len - 1)
    first = pos == 0
    last = pos == t_len - 1
    for s in range(n_sub):
        ab = jnp.dot(x_ext, w_s[:, 2 * s * sub:(2 * s + 2) * sub], preferred_element_type=f32)
        taps = jnp.concatenate([ca_ref[:, s * sub:(s + 1) * sub], cb_ref[:, s * sub:(s + 1) * sub]], axis=1)
        before = jnp.where(first, 0.0, pltpu.roll(ab, 1, 0)[HALO:HALO + FFN_TM])
        after = jnp.where(last, 0.0, pltpu.roll(ab, rows - 1, 0)[HALO:HALO + FFN_TM])
        y = before * taps[0:1] + ab[HALO:HALO + FFN_TM] * taps[1:2] + after * taps[2:3]
        o_ref[:, s * sub:(s + 1) * sub] = (_silu(y[:, 0:sub]) * y[:, sub:]).astype(o_ref.dtype)


def _ffn_up_glu(h, w_up, ffn_conv):
    nb = D_FF // FFN_TC
    per = FFN_TM // HALO
    last = N_TOK // HALO - 1
    return pl.pallas_call(
        _ffn_up_kernel,
        grid=(nb, N_TOK // FFN_TM),
        in_specs=[pl.BlockSpec((FFN_TM, D_MODEL), lambda j, i: (i, 0)),
                  pl.BlockSpec((HALO, D_MODEL), lambda j, i: (jnp.maximum(i * per - 1, 0), 0)),
                  pl.BlockSpec((HALO, D_MODEL), lambda j, i: (jnp.minimum((i + 1) * per, last), 0)),
                  pl.BlockSpec((D_MODEL, FFN_TC), lambda j, i: (0, j)),
                  pl.BlockSpec((D_MODEL, FFN_TC), lambda j, i: (0, j + nb)),
                  pl.BlockSpec((3, FFN_TC), lambda j, i: (0, j)),
                  pl.BlockSpec((3, FFN_TC), lambda j, i: (0, j + nb))],
        out_specs=pl.BlockSpec((FFN_TM, FFN_TC), lambda j, i: (i, j)),
        out_shape=jax.ShapeDtypeStruct((N_TOK, D_FF), bf16),
        scratch_shapes=[pltpu.VMEM((D_MODEL, 2 * FFN_TC), bf16)],
        compiler_params=_cparams(2),
        name="ffn_up_glu",
    )(h, h, h, w_up, w_up, ffn_conv, ffn_conv)


GATE_LANES = 4 * GDN_HEADS
CHUNKS_PER_TILE = SEQ_TILE // GDN_CHUNK


def _gate_prep_kernel(u_ref, alog_ref, dtb_ref, cols_ref, rows_ref):
    u = u_ref[...]
    lane = lax.broadcasted_iota(jnp.int32, (1, B_W), 1)
    x = u + dtb_ref[...]
    softplus = jnp.maximum(x, 0.0) + jnp.log1p(jnp.exp(-jnp.abs(x)))
    g = -jnp.exp(alog_ref[...]) * softplus
    r = lax.broadcasted_iota(jnp.int32, (SEQ_TILE, SEQ_TILE), 0)
    c = lax.broadcasted_iota(jnp.int32, (SEQ_TILE, SEQ_TILE), 1)
    shift = int(math.log2(GDN_CHUNK))
    same = (r >> shift) == (c >> shift)
    tri_f = jnp.where(same & (c <= r), 1.0, 0.0)
    tri_b = jnp.where(same & (c >= r), 1.0, 0.0)
    gc_f = jnp.dot(tri_f, g, preferred_element_type=f32, precision=lax.Precision.HIGHEST)
    gc_b = jnp.dot(tri_b, g, preferred_element_type=f32, precision=lax.Precision.HIGHEST)
    cols = jnp.where(lane < 2 * GDN_HEADS, jax.nn.sigmoid(u), jnp.where(lane < 3 * GDN_HEADS, gc_f, gc_b))
    cols_ref[...] = cols
    for j in range(CHUNKS_PER_TILE):
        t = cols[j * GDN_CHUNK:(j + 1) * GDN_CHUNK, :].T
        rows_ref[j] = t[0:GATE_LANES, :]


def _gate_prep(u_b, a_log, dt_bias):
    pad = lambda p: jnp.zeros((1, B_W), f32).at[0, 2 * GDN_HEADS:GATE_LANES].set(p.reshape(-1))
    n_tiles = N_TOK // SEQ_TILE
    return pl.pallas_call(
        _gate_prep_kernel,
        grid=(n_tiles,),
        in_specs=[pl.BlockSpec((SEQ_TILE, B_W), lambda m: (m, 0)),
                  pl.BlockSpec((1, B_W), lambda m: (0, 0)),
                  pl.BlockSpec((1, B_W), lambda m: (0, 0))],
        out_specs=[pl.BlockSpec((SEQ_TILE, B_W), lambda m: (m, 0)),
                   pl.BlockSpec((CHUNKS_PER_TILE, GATE_LANES, GDN_CHUNK), lambda m: (m, 0, 0))],
        out_shape=[jax.ShapeDtypeStruct((N_TOK, B_W), f32),
                   jax.ShapeDtypeStruct((N_TOK // GDN_CHUNK, GATE_LANES, GDN_CHUNK), f32)],
        compiler_params=_cparams(1),
        name="gdn_gate_prep",
    )(u_b, pad(a_log), pad(dt_bias))


def _gdn_conv_silu(x_ref, w_ref, t_len):
    x = x_ref[...].astype(f32)
    row = lax.broadcasted_iota(jnp.int32, (t_len, 1), 0)
    before = jnp.where(row == 0, 0.0, pltpu.roll(x, 1, 0))
    after = jnp.where(row == t_len - 1, 0.0, pltpu.roll(x, t_len - 1, 0))
    return _silu(before * w_ref[0:1, :] + x * w_ref[1:2, :] + after * w_ref[2:3, :])


def _l2norm_heads(x, n_heads):
    hd = GDN_HEAD_DIM
    parts = []
    for h in range(n_heads):
        xh = x[:, h * hd:(h + 1) * hd]
        parts.append(xh * lax.rsqrt(jnp.sum(xh * xh, axis=-1, keepdims=True) + EPS))
    return parts[0] if n_heads == 1 else jnp.concatenate(parts, axis=1)


GDN_GROUP = 4
GDN_HEADS_PER_STEP = 2
GDN_STEP_ROWS = GDN_HEAD_DIM + GDN_CHUNK


def _gdn_prepare(grp, head0, q_s, k_s, v_s, gcol_ref, grow_ref, pq_s, b_s, gl_s, o_s):
    cs, hd = GDN_CHUNK, GDN_HEAD_DIM
    rows = GDN_GROUP * cs
    r0 = pl.multiple_of(grp * rows, rows)
    lane = lax.broadcasted_iota(jnp.int32, (1, B_W), 1)
    ri = lax.broadcasted_iota(jnp.int32, (cs, cs), 0)
    ci = lax.broadcasted_iota(jnp.int32, (cs, cs), 1)
    blk = ri ^ ci
    probs = []
    for j in range(GDN_GROUP):
        rj = r0 + j * cs
        gates = gcol_ref[pl.ds(rj, cs), :]
        pick = lambda idx, gates=gates: jnp.sum(jnp.where(lane == idx, gates, 0.0), axis=1, keepdims=True)
        for hh in range(GDN_HEADS_PER_STEP):
            cols = slice(hh * hd, (hh + 1) * hd)
            head = head0 + hh
            q = q_s[pl.ds(rj, cs), cols]
            k = k_s[pl.ds(rj, cs), cols]
            v = v_s[pl.ds(rj, cs), cols]
            kk = _dot_nt(k, k)
            qk = _dot_nt(q, k)
            for direction in range(2):
                beta = pick(direction * GDN_HEADS + head)
                dec_lane = (2 + direction) * GDN_HEADS + head
                gc = pick(dec_lane)
                gr = grow_ref[grp * GDN_GROUP + j, pl.ds(dec_lane, 1), :]
                if direction == 0:
                    incl, strict = ri >= ci, ri > ci
                    g_tot = gc[cs - 1:cs, :]
                else:
                    incl, strict = ri <= ci, ri < ci
                    g_tot = gc[0:1, :]
                decay = jnp.exp(jnp.where(incl, gc - gr, -jnp.inf))
                probs.append(dict(j=j, hh=hh, d=direction, q=q, k=k, v=v, beta=beta, gc=gc, g_tot=g_tot,
                                  qk=qk * decay, lmat=jnp.where(strict, kk * decay * beta, 0.0)))
    for p in probs:
        p["n"] = -jnp.where(blk < 2, p["lmat"], 0.0)
    for level in range(1, int(math.log2(cs))):
        mask = (blk >= (1 << level)) & (blk < (2 << level))
        for p in probs:
            c_k = jnp.where(mask, p["lmat"], 0.0)
            y = c_k + _dot(c_k, p["n"])
            p["n"] = p["n"] - y - _dot(p["n"], y)
    for p in probs:
        j, hh, d = p["j"], p["hh"], p["d"]
        e_gc = jnp.exp(p["gc"])
        vb = p["v"] * p["beta"]
        kbg = p["k"] * (p["beta"] * e_gc)
        uw = _dot(p["n"], jnp.concatenate([vb, kbg], axis=1))
        wu = jnp.concatenate([kbg + uw[:, hd:], vb + uw[:, 0:hd]], axis=1)
        kd = p["k"] * jnp.exp(p["g_tot"] - p["gc"])
        top = _dot_tn(kd, wu)
        bot = _dot(p["qk"], wu)
        c = grp * GDN_GROUP + j
        pq_s[hh, d, c, 0:hd, :] = top[:, 0:hd].astype(pq_s.dtype)
        pq_s[hh, d, c, hd:, :] = (p["q"] * e_gc - bot[:, 0:hd]).astype(pq_s.dtype)
        b_s[hh, d, c] = top[:, hd:]
        o_s[hh, d, pl.ds(r0 + j * cs, cs), :] = bot[:, hd:]
        gl_s[hh, d, c] = jnp.broadcast_to(jnp.exp(p["g_tot"]), (1, hd))


def _gdn_scan_step(c, hh, d, state, pq_s, b_s, gl_s, o_s):
    hd = GDN_HEAD_DIM
    r0 = pl.multiple_of(c * GDN_CHUNK, GDN_CHUNK)
    r = _dot(pq_s[hh, d, c], state)
    o_s[hh, d, pl.ds(r0, GDN_CHUNK), :] += r[hd:]
    return state * gl_s[hh, d, c] - r[0:hd] + b_s[hh, d, c]


def _gdn_kernel(q_ref, k_ref, v_ref, z_ref, wq_ref, wk_ref, wv_ref, gcol_ref, grow_ref, ng_ref, s0_ref,
                y_ref, sfin_ref, q_s, k_s, v_s, o_s, pq_s, b_s, gl_s, *, t_len):
    hp, hd = GDN_HEADS_PER_STEP, GDN_HEAD_DIM
    head0 = pl.program_id(1) * hp
    n_chunks = t_len // GDN_CHUNK
    q_s[...] = _l2norm_heads(_gdn_conv_silu(q_ref, wq_ref, t_len), hp) * (hd ** -0.5)
    k_s[...] = _l2norm_heads(_gdn_conv_silu(k_ref, wk_ref, t_len), hp)
    v_s[...] = _gdn_conv_silu(v_ref, wv_ref, t_len)

    def prepare(grp, carry):
        _gdn_prepare(grp, head0, q_s, k_s, v_s, gcol_ref, grow_ref, pq_s, b_s, gl_s, o_s)
        return carry

    lax.fori_loop(0, n_chunks // GDN_GROUP, prepare, 0)

    def scan(i, states):
        new = []
        for hh in range(hp):
            new.append(_gdn_scan_step(i, hh, 0, states[2 * hh], pq_s, b_s, gl_s, o_s))
            new.append(_gdn_scan_step(n_chunks - 1 - i, hh, 1, states[2 * hh + 1], pq_s, b_s, gl_s, o_s))
        return tuple(new)

    init = tuple(s0_ref[d, hh] for hh in range(hp) for d in range(2))
    final = lax.fori_loop(0, n_chunks, scan, init)
    outs = []
    for hh in range(hp):
        sfin_ref[0, hh] = final[2 * hh]
        sfin_ref[1, hh] = final[2 * hh + 1]
        o = o_s[hh, 0] + o_s[hh, 1]
        outs.append(o * lax.rsqrt(jnp.mean(o * o, axis=-1, keepdims=True) + EPS) * ng_ref[...])
    o = outs[0] if hp == 1 else jnp.concatenate(outs, axis=1)
    y_ref[...] = (o * _silu(z_ref[...].astype(f32))).astype(y_ref.dtype)


def _gdn(u_a, gcols, grows, conv_w, norm_g, s0, t_len, n_seq, row_off):
    hd = GDN_HEAD_DIM
    hp = GDN_HEADS_PER_STEP
    wd = hd * hp
    nb = GDN_HEADS // hp
    blk0 = row_off // t_len
    c_q = POOL_W // wd
    col = lambda base: pl.BlockSpec((t_len, wd), lambda s, h: (blk0 + s, base + h))
    cw = lambda base: pl.BlockSpec((3, wd), lambda s, h: (0, base + h))
    n_chunks = t_len // GDN_CHUNK
    state_spec = pl.BlockSpec((None, 2, hp, hd, hd), lambda s, h: (s, 0, h, 0, 0))
    return pl.pallas_call(
        functools.partial(_gdn_kernel, t_len=t_len),
        grid=(n_seq, nb),
        in_specs=[col(c_q), col(c_q + nb), col(c_q + 2 * nb), col(c_q + 3 * nb),
                  cw(0), cw(nb), cw(2 * nb),
                  pl.BlockSpec((t_len, B_W), lambda s, h: (blk0 + s, 0)),
                  pl.BlockSpec((n_chunks, GATE_LANES, GDN_CHUNK), lambda s, h: (blk0 + s, 0, 0)),
                  pl.BlockSpec((1, hd), lambda s, h: (0, 0)),
                  state_spec],
        out_specs=[pl.BlockSpec((t_len, wd), lambda s, h: (s, h)), state_spec],
        out_shape=[jax.ShapeDtypeStruct((n_seq * t_len, GDN_W), bf16),
                   jax.ShapeDtypeStruct((n_seq, 2, GDN_HEADS, hd, hd), f32)],
        scratch_shapes=[pltpu.VMEM((t_len, wd), f32)] * 3
        + [pltpu.VMEM((hp, 2, t_len, hd), f32),
           pltpu.VMEM((hp, 2, n_chunks, GDN_STEP_ROWS, hd), bf16),
           pltpu.VMEM((hp, 2, n_chunks, hd, hd), f32),
           pltpu.VMEM((hp, 2, n_chunks, 1, hd), f32)],
        compiler_params=_cparams(2),
        name=f"gdn_t{t_len}",
    )(u_a, u_a, u_a, u_a, conv_w, conv_w, conv_w, gcols, grows, norm_g.reshape(1, hd), s0)


def _ctx_attn_kernel(x_ref, o_ref):
    hd = NA_HEAD_DIM
    outs = []
    for h in range(NA_HEADS):
        q = x_ref[:, h * hd:(h + 1) * hd]
        k = x_ref[:, NA_W + h * hd:NA_W + (h + 1) * hd]
        v = x_ref[:, 2 * NA_W + h * hd:2 * NA_W + (h + 1) * hd]
        s = _dot_nt(q, k) * (hd ** -0.5)
        p = jnp.exp(s - jnp.max(s, axis=-1, keepdims=True))
        outs.append(_dot(p, v) / jnp.sum(p, axis=-1, keepdims=True))
    o_ref[...] = jnp.concatenate(outs, axis=1).astype(o_ref.dtype)


def _ctx_attention(u_c):
    return pl.pallas_call(
        _ctx_attn_kernel,
        grid=(BATCH,),
        in_specs=[pl.BlockSpec((SEQ, C_W), lambda b: (b, 0))],
        out_specs=pl.BlockSpec((SEQ, NA_W), lambda b: (b, 0)),
        out_shape=jax.ShapeDtypeStruct((N_CTX, NA_W), bf16),
        compiler_params=_cparams(1),
        name="ctx_attention",
    )(u_c)


HEADS_PER_STEP = 128 // NA_HEAD_DIM
NB_KEYS = NA_WR * GRID_W


def _na_kernel(q_ref, k_ref, v_ref, kc_ref, vc_ref, bias_ref, o_ref, k_s, v_s, kc_s, vc_s):
    hd = NA_HEAD_DIM
    scale = hd ** -0.5
    k_s[...] = k_ref[...].astype(bf16)
    v_s[...] = v_ref[...].astype(bf16)
    kc_s[...] = kc_ref[...].astype(bf16)
    vc_s[...] = vc_ref[...].astype(bf16)
    lane = lax.broadcasted_iota(jnp.int32, (1, HEADS_PER_STEP * hd), 1)

    def body(r, carry):
        start = jnp.clip(r - NA_WR // 2, 0, GRID_ROWS - NA_WR)
        case = start - r + NA_WR - 1
        q0 = pl.multiple_of(r * GRID_W, GRID_W)
        k0 = pl.multiple_of(start * GRID_W, GRID_W)
        q = q_ref[pl.ds(q0, GRID_W), :] * scale
        k_nb = k_s[pl.ds(k0, NB_KEYS), :]
        v_nb = v_s[pl.ds(k0, NB_KEYS), :]
        out = None
        for hh in range(HEADS_PER_STEP):
            own = (lane >= hh * hd) & (lane < (hh + 1) * hd)
            qh = jnp.where(own, q, 0.0).astype(bf16)
            s_nb = _dot_nt(qh, k_nb) + bias_ref[hh, case]
            s_cx = _dot_nt(qh, kc_s[...])
            mx = jnp.maximum(jnp.max(s_nb, axis=-1, keepdims=True), jnp.max(s_cx, axis=-1, keepdims=True))
            p_nb = jnp.exp(s_nb - mx)
            p_cx = jnp.exp(s_cx - mx)
            den = jnp.sum(p_nb, axis=-1, keepdims=True) + jnp.sum(p_cx, axis=-1, keepdims=True)
            o = (_dot(p_nb, v_nb) + _dot(p_cx, vc_s[...])) / den
            out = o if out is None else jnp.where(own, o, out)
        o_ref[pl.ds(q0, GRID_W), :] = out.astype(o_ref.dtype)
        return carry

    lax.fori_loop(0, GRID_ROWS, body, 0, unroll=2)


def _na_bias_tiles(rpb):
    col = np.arange(GRID_W)
    col_start = np.clip(col - NA_WC // 2, 0, GRID_W - NA_WC)
    col_mask = (col[None, :] >= col_start[:, None]) & (col[None, :] < col_start[:, None] + NA_WC)
    d_col = np.clip(col[None, :] - col[:, None], -(NA_WC - 1), NA_WC - 1) + NA_WC - 1
    table = jnp.where(col_mask[None, None], rpb[:, :, d_col], NEG_BIG)
    tiles = [table[:, case:case + NA_WR].transpose(0, 2, 1, 3).reshape(NA_HEADS, GRID_W, NB_KEYS)
             for case in range(NA_WR)]
    return jnp.stack(tiles, axis=1).astype(f32)


def _na_attention(u_c, k_ctx, v_ctx, bias_tiles):
    blk0 = N_CTX // DEC_SEQ
    nhp = NA_HEADS // HEADS_PER_STEP
    col = lambda base: pl.BlockSpec((DEC_SEQ, 128), lambda b, p: (blk0 + b, base + p))
    cache = pl.BlockSpec((None, PAST_LEN, 128), lambda b, p: (b, 0, p))
    return pl.pallas_call(
        _na_kernel,
        grid=(DEC_BATCH, nhp),
        in_specs=[col(0), col(nhp), col(2 * nhp), cache, cache,
                  pl.BlockSpec((HEADS_PER_STEP, NA_WR, GRID_W, NB_KEYS), lambda b, p: (p, 0, 0, 0))],
        out_specs=pl.BlockSpec((DEC_SEQ, 128), lambda b, p: (b, p)),
        out_shape=jax.ShapeDtypeStruct((N_LAT, NA_W), bf16),
        scratch_shapes=[pltpu.VMEM((DEC_SEQ, 128), bf16)] * 2 + [pltpu.VMEM((PAST_LEN, 128), bf16)] * 2,
        compiler_params=_cparams(2),
        name="na_attention",
    )(u_c, u_c, u_c, k_ctx, v_ctx, bias_tiles)


def _final_norm_kernel(x_ref, g_ref, o_ref):
    x = x_ref[...]
    o_ref[...] = x * lax.rsqrt(jnp.mean(x * x, axis=-1, keepdims=True) + EPS) * g_ref[...]


def _final_norm(x, g, n_rows, row_off):
    tm = 512
    blk0 = row_off // tm
    return pl.pallas_call(
        _final_norm_kernel,
        grid=(n_rows // tm,),
        in_specs=[pl.BlockSpec((tm, D_MODEL), lambda m: (blk0 + m, 0)),
                  pl.BlockSpec((1, D_MODEL), lambda m: (0, 0))],
        out_specs=pl.BlockSpec((tm, D_MODEL), lambda m: (m, 0)),
        out_shape=jax.ShapeDtypeStruct((n_rows, D_MODEL), f32),
        compiler_params=_cparams(1),
        name="final_norm",
    )(x, g.reshape(1, D_MODEL))


def kernel(x_prompt, x_sample, cache_na_k, cache_na_v, state_gdn, c, c_ctx, w_ada, b_ada, g_norm1, w_in, pool_w, pool_scale, gdn_conv, gdn_a_log, gdn_dt_bias, gdn_norm_g, na_rpb, w_branch_pool, w_branch_gdn, w_branch_na, w_out, g_norm2, w_up, ffn_conv, w_down, g_final):
    x = jnp.concatenate([x_prompt.reshape(N_CTX, D_MODEL), x_sample.reshape(N_LAT, D_MODEL)], axis=0)

    cvec = jnp.zeros((8, D_MODEL), f32).at[0].set(c_ctx).at[1:1 + DEC_BATCH].set(c)
    mods = _modulation(cvec, w_ada, b_ada)
    group_row = np.array([0] * (N_CTX // ROW_GROUP) + list(range(1, 1 + DEC_BATCH)))
    mods = mods[:, group_row].reshape(DEPTH, N_GROUPS, 6, 1, D_MODEL).transpose(0, 2, 1, 3, 4)

    zero_state = jnp.zeros((BATCH, 2, GDN_HEADS, GDN_HEAD_DIM, GDN_HEAD_DIM), f32)
    new_k, new_v, new_s = [], [], []
    for l in range(DEPTH):
        sh1, sc1, g1, sh2, sc2, g2 = (mods[l, j] for j in range(6))
        w_l = w_in[l]
        o_b = A_W
        o_c = o_b + GATE_LANES
        o_g = o_c + C_W
        w_a = w_l[:, :A_W].astype(bf16)
        w_b = jnp.pad(w_l[:, o_b:o_c], ((0, 0), (0, B_W - GATE_LANES))).astype(bf16)
        w_c = w_l[:, o_c:o_g].astype(bf16)
        w_g = w_l[:, o_g:].astype(bf16)

        h = _norm_mod(x, g_norm1[l], sc1, sh1)
        u_a = _matmul(h, w_a, bf16, "in_proj_mix")
        u_b = _matmul(h, w_b, f32, "in_proj_gates")
        u_c = _matmul(h, w_c, f32, "in_proj_attn")
        u_g = _matmul(h, w_g, bf16, "in_proj_branch_gates")

        y_pool = _pool(u_a, pool_w[l].astype(bf16), pool_scale[l])

        gcols, grows = _gate_prep(u_b, gdn_a_log[l], gdn_dt_bias[l])
        y_gdn_ctx, s_ctx = _gdn(u_a, gcols, grows, gdn_conv[l], gdn_norm_g[l], zero_state, SEQ, BATCH, 0)
        y_gdn_lat, _ = _gdn(u_a, gcols, grows, gdn_conv[l], gdn_norm_g[l], state_gdn[:, l], DEC_SEQ, DEC_BATCH, N_CTX)

        y_na_ctx = _ctx_attention(u_c)
        y_na_lat = _na_attention(u_c, cache_na_k[:, l].reshape(DEC_BATCH, PAST_LEN, NA_W),
                                 cache_na_v[:, l].reshape(DEC_BATCH, PAST_LEN, NA_W), _na_bias_tiles(na_rpb[l]))

        merged = _merge(y_pool, y_gdn_ctx, y_gdn_lat, y_na_ctx, y_na_lat, w_branch_pool[l].astype(bf16),
                        w_branch_gdn[l].astype(bf16), w_branch_na[l].astype(bf16), u_g)
        x = _matmul_residual(merged, w_out[l].astype(bf16), x, g1, "out_proj")

        h = _norm_mod(x, g_norm2[l], sc2, sh2)
        act = _ffn_up_glu(h, w_up[l], ffn_conv[l])
        x = _matmul_residual(act, w_down[l].astype(bf16), x, g2, "ffn_down")

        new_k.append(u_c[:N_CTX, NA_W:2 * NA_W].reshape(BATCH, SEQ, NA_HEADS, NA_HEAD_DIM))
        new_v.append(u_c[:N_CTX, 2 * NA_W:].reshape(BATCH, SEQ, NA_HEADS, NA_HEAD_DIM))
        new_s.append(s_ctx)

    y_prompt = _final_norm(x, g_final, N_CTX, 0).reshape(BATCH, SEQ, D_MODEL)
    y_sample = _final_norm(x, g_final, N_LAT, N_CTX).reshape(DEC_BATCH, DEC_SEQ, D_MODEL)
    return (y_prompt, y_sample, jnp.stack(new_k, axis=1), jnp.stack(new_v, axis=1), jnp.stack(new_s, axis=1))
```

```python
import functools
import math

import numpy as np
import jax
import jax.numpy as jnp
from jax import lax
from jax.experimental import pallas as pl
from jax.experimental.pallas import tpu as pltpu

D_MODEL = 2048
BATCH = 16
SEQ = 256
DEPTH = 2
DEC_BATCH = 2
DEC_SEQ = 2048
PAST_LEN = 512
GRID_W = 64
POOL_GROUPS = 4
POOL_GC = 128
POOL_W = POOL_GROUPS * POOL_GC
POOL_WINDOWS = (2, 4, 8, 16)
GDN_HEADS = 8
GDN_HEAD_DIM = 128
GDN_W = GDN_HEADS * GDN_HEAD_DIM
GDN_CHUNK = 64
NA_HEADS = 8
NA_HEAD_DIM = 64
NA_W = NA_HEADS * NA_HEAD_DIM
NA_WR = 8
NA_WC = 16
D_FF = 5632
EPS = 1e-6

N_CTX = BATCH * SEQ
N_LAT = DEC_BATCH * DEC_SEQ
N_TOK = N_CTX + N_LAT
ROW_GROUP = DEC_SEQ
N_GROUPS = N_TOK // ROW_GROUP
SEQ_TILE = SEQ
HALO = 16
GRID_ROWS = DEC_SEQ // GRID_W
NEG_BIG = -1e30

A_W = POOL_W + 4 * GDN_W
B_W = 128
C_W = 3 * NA_W
G_W = 3 * D_MODEL
IN_PAD_B = 1024
IN_OFF_B = A_W
IN_OFF_C = IN_OFF_B + IN_PAD_B
IN_OFF_G = IN_OFF_C + C_W

VMEM_LIMIT = 48 * 1024 * 1024

f32 = jnp.float32
bf16 = jnp.bfloat16


def _cparams(n_axes):
    return pltpu.CompilerParams(dimension_semantics=("arbitrary",) * n_axes, vmem_limit_bytes=VMEM_LIMIT)


def _dot(a, b):
    return jnp.dot(a.astype(bf16), b.astype(bf16), preferred_element_type=f32)


def _dot_nt(a, b):
    return lax.dot_general(a.astype(bf16), b.astype(bf16), (((1,), (1,)), ((), ())), preferred_element_type=f32)


def _dot_tn(a, b):
    return lax.dot_general(a.astype(bf16), b.astype(bf16), (((0,), (0,)), ((), ())), preferred_element_type=f32)


def _silu(x):
    return x * jax.nn.sigmoid(x)


def _col_tile(n, cap=1536):
    best = 128
    for t in range(128, cap + 1, 128):
        if n % t == 0:
            best = t
    return best


def _mod_kernel(c_ref, w_ref, b_ref, o_ref):
    a = _silu(c_ref[...])
    o_ref[...] = _dot(a, w_ref[...]) + b_ref[...]


def _modulation(cvec, w_ada, b_ada):
    tn = 1024
    n = 6 * D_MODEL
    return pl.pallas_call(
        _mod_kernel,
        grid=(DEPTH, n // tn),
        in_specs=[
            pl.BlockSpec((8, D_MODEL), lambda l, j: (0, 0)),
            pl.BlockSpec((None, D_MODEL, tn), lambda l, j: (l, 0, j)),
            pl.BlockSpec((None, 1, tn), lambda l, j: (l, 0, j)),
        ],
        out_specs=pl.BlockSpec((None, 8, tn), lambda l, j: (l, 0, j)),
        out_shape=jax.ShapeDtypeStruct((DEPTH, 8, n), f32),
        compiler_params=_cparams(2),
        name="modulation",
    )(cvec, w_ada, b_ada.reshape(DEPTH, 1, n))


def _mod_spec(tm, tn, col_of):
    return pl.BlockSpec((None, 1, tn), lambda m, n: ((m * tm) // ROW_GROUP, 0, col_of(n)))


def _norm_mod_kernel(x_ref, g_ref, sc_ref, sh_ref, o_ref):
    x = x_ref[...]
    y = x * lax.rsqrt(jnp.mean(x * x, axis=-1, keepdims=True) + EPS) * g_ref[...]
    o_ref[...] = (y * (1.0 + sc_ref[...]) + sh_ref[...]).astype(o_ref.dtype)


def _norm_mod(x, g, scale, shift):
    tm = 512
    mod = pl.BlockSpec((None, 1, D_MODEL), lambda m: ((m * tm) // ROW_GROUP, 0, 0))
    return pl.pallas_call(
        _norm_mod_kernel,
        grid=(N_TOK // tm,),
        in_specs=[
            pl.BlockSpec((tm, D_MODEL), lambda m: (m, 0)),
            pl.BlockSpec((1, D_MODEL), lambda m: (0, 0)),
            mod, mod,
        ],
        out_specs=pl.BlockSpec((tm, D_MODEL), lambda m: (m, 0)),
        out_shape=jax.ShapeDtypeStruct((N_TOK, D_MODEL), bf16),
        compiler_params=_cparams(1),
        name="norm_mod",
    )(x, g.reshape(1, D_MODEL), scale, shift)


def _mm_kernel(x_ref, w_ref, o_ref):
    o_ref[...] = jnp.dot(x_ref[...], w_ref[...], preferred_element_type=f32).astype(o_ref.dtype)


def _matmul(x, w, out_dtype, name, col0=0, n=None):
    m, k = x.shape
    n = w.shape[1] if n is None else n
    tm = 1024
    tn = _col_tile(math.gcd(n, col0) if col0 else n, 1024)
    j0 = col0 // tn
    return pl.pallas_call(
        _mm_kernel,
        grid=(m // tm, n // tn),
        in_specs=[
            pl.BlockSpec((tm, k), lambda i, j: (i, 0)),
            pl.BlockSpec((k, tn), lambda i, j: (0, j0 + j)),
        ],
        out_specs=pl.BlockSpec((tm, tn), lambda i, j: (i, j)),
        out_shape=jax.ShapeDtypeStruct((m, n), out_dtype),
        compiler_params=_cparams(2),
        name=name,
    )(x, w)


def _mm_res_kernel(a_ref, w_ref, x_ref, g_ref, o_ref):
    y = jnp.dot(a_ref[...], w_ref[...], preferred_element_type=f32)
    o_ref[...] = x_ref[...] + g_ref[...] * y


def _matmul_residual(a, w, x, gate, name):
    m, k = a.shape
    n = w.shape[1]
    tm = 1024
    tn = 1024 if k <= D_MODEL else 512
    return pl.pallas_call(
        _mm_res_kernel,
        grid=(m // tm, n // tn),
        in_specs=[
            pl.BlockSpec((tm, k), lambda i, j: (i, 0)),
            pl.BlockSpec((k, tn), lambda i, j: (0, j)),
            pl.BlockSpec((tm, tn), lambda i, j: (i, j)),
            _mod_spec(tm, tn, lambda j: j),
        ],
        out_specs=pl.BlockSpec((tm, tn), lambda i, j: (i, j)),
        out_shape=jax.ShapeDtypeStruct((m, n), f32),
        compiler_params=_cparams(2),
        name=name,
    )(a, w, x, gate)


MERGE_TM = 1024


def _merge_kernel(yp_ref, ygc_ref, ygl_ref, ync_ref, ynl_ref, wp_ref, wg_ref, wn_ref, g0_ref, g1_ref, g2_ref, o_ref):
    is_ctx = pl.program_id(0) < N_CTX // MERGE_TM

    def branch(y, w_ref, g_ref):
        return jax.nn.sigmoid(g_ref[...].astype(f32)) * jnp.dot(y, w_ref[...], preferred_element_type=f32)

    y_gdn = jnp.where(is_ctx, ygc_ref[...], ygl_ref[...])
    y_na = jnp.where(is_ctx, ync_ref[...], ynl_ref[...])
    acc = branch(yp_ref[...], wp_ref, g0_ref) + branch(y_gdn, wg_ref, g1_ref) + branch(y_na, wn_ref, g2_ref)
    o_ref[...] = acc.astype(o_ref.dtype)


def _merge(y_pool, y_gdn_ctx, y_gdn_lat, y_na_ctx, y_na_lat, w_pool, w_gdn, w_na, u_gate):
    tm = MERGE_TM
    tn = 1024
    nb = D_MODEL // tn
    n_ctx = N_CTX // tm
    row = lambda width: pl.BlockSpec((tm, width), lambda i, j: (i, 0))
    ctx = lambda width: pl.BlockSpec((tm, width), lambda i, j: (jnp.minimum(i, n_ctx - 1), 0))
    lat = lambda width: pl.BlockSpec((tm, width), lambda i, j: (jnp.maximum(i - n_ctx, 0), 0))
    wcol = lambda kdim: pl.BlockSpec((kdim, tn), lambda i, j: (0, j))
    gate = lambda b: pl.BlockSpec((tm, tn), lambda i, j: (i, j + b * nb))
    return pl.pallas_call(
        _merge_kernel,
        grid=(N_TOK // tm, nb),
        in_specs=[row(POOL_W), ctx(GDN_W), lat(GDN_W), ctx(NA_W), lat(NA_W),
                  wcol(POOL_W), wcol(GDN_W), wcol(NA_W), gate(0), gate(1), gate(2)],
        out_specs=pl.BlockSpec((tm, tn), lambda i, j: (i, j)),
        out_shape=jax.ShapeDtypeStruct((N_TOK, D_MODEL), bf16),
        compiler_params=_cparams(2),
        name="merge",
    )(y_pool, y_gdn_ctx, y_gdn_lat, y_na_ctx, y_na_lat, w_pool, w_gdn, w_na, u_gate, u_gate, u_gate)


def _tile_seq_len(m):
    return jnp.where(m * SEQ_TILE < N_CTX, SEQ, DEC_SEQ)


def _halo_specs(width, col_of):
    per = SEQ_TILE // HALO
    last = N_TOK // HALO - 1
    main = pl.BlockSpec((SEQ_TILE, width), lambda m, c: (m, col_of(c)))
    prev = pl.BlockSpec((HALO, width), lambda m, c: (jnp.maximum(m * per - 1, 0), col_of(c)))
    nxt = pl.BlockSpec((HALO, width), lambda m, c: (jnp.minimum((m + 1) * per, last), col_of(c)))
    return main, prev, nxt


def _tile_pos(m):
    t_len = _tile_seq_len(m)
    row = lax.broadcasted_iota(jnp.int32, (SEQ_TILE, 1), 0)
    return (m * SEQ_TILE + row) & (t_len - 1), t_len


def _pool_kernel(x_ref, xp_ref, xn_ref, w_ref, s_ref, o_ref, ext_ref):
    m = pl.program_id(0)
    pos, t_len = _tile_pos(m)
    first = ((m * SEQ_TILE) & (t_len - 1)) == 0
    last = (((m + 1) * SEQ_TILE) & (t_len - 1)) == 0
    x = x_ref[...].astype(f32)
    ext_ref[0:HALO, :] = jnp.where(first, 0.0, xp_ref[...].astype(f32))
    ext_ref[HALO:HALO + SEQ_TILE, :] = x
    ext_ref[HALO + SEQ_TILE:, :] = jnp.where(last, 0.0, xn_ref[...].astype(f32))
    outs = []
    for g, win in enumerate(POOL_WINDOWS):
        cols = slice(g * POOL_GC, (g + 1) * POOL_GC)
        acc = jnp.zeros((SEQ_TILE, POOL_GC), f32)
        for s in range(-(win // 2), win - win // 2):
            acc = acc + ext_ref[HALO + s:HALO + s + SEQ_TILE, cols]
        lo = jnp.maximum(pos - win // 2, 0)
        hi = jnp.minimum(pos + win - 1 - win // 2, t_len - 1)
        mean = acc / (hi - lo + 1).astype(f32)
        y = _dot(mean - x[:, cols], w_ref[g]) * s_ref[:, cols]
        outs.append(y)
    o_ref[...] = jnp.concatenate(outs, axis=1).astype(o_ref.dtype)


def _pool(u_a, pool_w, pool_scale):
    main, prev, nxt = _halo_specs(POOL_W, lambda c: 0)
    return pl.pallas_call(
        _pool_kernel,
        grid=(N_TOK // SEQ_TILE, 1),
        in_specs=[main, prev, nxt,
                  pl.BlockSpec((POOL_GROUPS, POOL_GC, POOL_GC), lambda m, c: (0, 0, 0)),
                  pl.BlockSpec((1, POOL_W), lambda m, c: (0, 0))],
        out_specs=pl.BlockSpec((SEQ_TILE, POOL_W), lambda m, c: (m, 0)),
        out_shape=jax.ShapeDtypeStruct((N_TOK, POOL_W), bf16),
        scratch_shapes=[pltpu.VMEM((SEQ_TILE + 2 * HALO, POOL_W), f32)],
        compiler_params=_cparams(2),
        name="pool",
    )(u_a, u_a, u_a, pool_w, pool_scale.reshape(1, POOL_W))


FFN_TM = 1024
FFN_TC = 512
FFN_SUB = 128


def _ffn_up_kernel(x_ref, xp_ref, xn_ref, wa_ref, wb_ref, ca_ref, cb_ref, o_ref, w_s):
    i = pl.program_id(1)
    n_sub = FFN_TC // FFN_SUB
    sub = FFN_SUB

    @pl.when(i == 0)
    def _():
        for s in range(n_sub):
            w_s[:, 2 * s * sub:(2 * s + 1) * sub] = wa_ref[:, s * sub:(s + 1) * sub].astype(bf16)
            w_s[:, (2 * s + 1) * sub:(2 * s + 2) * sub] = wb_ref[:, s * sub:(s + 1) * sub].astype(bf16)

    rows = FFN_TM + 2 * HALO
    x_ext = jnp.concatenate([xp_ref[...], x_ref[...], xn_ref[...]], axis=0)
    t_len = jnp.where(i * FFN_TM < N_CTX, SEQ, DEC_SEQ)
    pos = (i * FFN_TM + lax.broadcasted_iota(jnp.int32, (FFN_TM, 1), 0)) & (t_len - 1)
    first = pos == 0
    last = pos == t_len - 1
    for s in range(n_sub):
        ab = jnp.dot(x_ext, w_s[:, 2 * s * sub:(2 * s + 2) * sub], preferred_element_type=f32)
        taps = jnp.concatenate([ca_ref[:, s * sub:(s + 1) * sub], cb_ref[:, s * sub:(s + 1) * sub]], axis=1)
        before = jnp.where(first, 0.0, pltpu.roll(ab, 1, 0)[HALO:HALO + FFN_TM])
        after = jnp.where(last, 0.0, pltpu.roll(ab, rows - 1, 0)[HALO:HALO + FFN_TM])
        y = before * taps[0:1] + ab[HALO:HALO + FFN_TM] * taps[1:2] + after * taps[2:3]
        o_ref[:, s * sub:(s + 1) * sub] = (_silu(y[:, 0:sub]) * y[:, sub:]).astype(o_ref.dtype)


def _ffn_up_glu(h, w_up, ffn_conv):
    nb = D_FF // FFN_TC
    per = FFN_TM // HALO
    last = N_TOK // HALO - 1
    return pl.pallas_call(
        _ffn_up_kernel,
        grid=(nb, N_TOK // FFN_TM),
        in_specs=[pl.BlockSpec((FFN_TM, D_MODEL), lambda j, i: (i, 0)),
                  pl.BlockSpec((HALO, D_MODEL), lambda j, i: (jnp.maximum(i * per - 1, 0), 0)),
                  pl.BlockSpec((HALO, D_MODEL), lambda j, i: (jnp.minimum((i + 1) * per, last), 0)),
                  pl.BlockSpec((D_MODEL, FFN_TC), lambda j, i: (0, j)),
                  pl.BlockSpec((D_MODEL, FFN_TC), lambda j, i: (0, j + nb)),
                  pl.BlockSpec((3, FFN_TC), lambda j, i: (0, j)),
                  pl.BlockSpec((3, FFN_TC), lambda j, i: (0, j + nb))],
        out_specs=pl.BlockSpec((FFN_TM, FFN_TC), lambda j, i: (i, j)),
        out_shape=jax.ShapeDtypeStruct((N_TOK, D_FF), bf16),
        scratch_shapes=[pltpu.VMEM((D_MODEL, 2 * FFN_TC), bf16)],
        compiler_params=_cparams(2),
        name="ffn_up_glu",
    )(h, h, h, w_up, w_up, ffn_conv, ffn_conv)


GATE_LANES = 4 * GDN_HEADS
CHUNKS_PER_TILE = SEQ_TILE // GDN_CHUNK


def _gate_prep_kernel(u_ref, alog_ref, dtb_ref, cols_ref, rows_ref):
    u = u_ref[...]
    lane = lax.broadcasted_iota(jnp.int32, (1, B_W), 1)
    x = u + dtb_ref[...]
    softplus = jnp.maximum(x, 0.0) + jnp.log1p(jnp.exp(-jnp.abs(x)))
    g = -jnp.exp(alog_ref[...]) * softplus
    r = lax.broadcasted_iota(jnp.int32, (SEQ_TILE, SEQ_TILE), 0)
    c = lax.broadcasted_iota(jnp.int32, (SEQ_TILE, SEQ_TILE), 1)
    shift = int(math.log2(GDN_CHUNK))
    same = (r >> shift) == (c >> shift)
    tri_f = jnp.where(same & (c <= r), 1.0, 0.0)
    tri_b = jnp.where(same & (c >= r), 1.0, 0.0)
    gc_f = jnp.dot(tri_f, g, preferred_element_type=f32, precision=lax.Precision.HIGHEST)
    gc_b = jnp.dot(tri_b, g, preferred_element_type=f32, precision=lax.Precision.HIGHEST)
    cols = jnp.where(lane < 2 * GDN_HEADS, jax.nn.sigmoid(u), jnp.where(lane < 3 * GDN_HEADS, gc_f, gc_b))
    cols_ref[...] = cols
    for j in range(CHUNKS_PER_TILE):
        t = cols[j * GDN_CHUNK:(j + 1) * GDN_CHUNK, :].T
        rows_ref[j] = t[0:GATE_LANES, :]


def _gate_prep(u_b, a_log, dt_bias):
    pad = lambda p: jnp.zeros((1, B_W), f32).at[0, 2 * GDN_HEADS:GATE_LANES].set(p.reshape(-1))
    n_tiles = N_TOK // SEQ_TILE
    return pl.pallas_call(
        _gate_prep_kernel,
        grid=(n_tiles,),
        in_specs=[pl.BlockSpec((SEQ_TILE, B_W), lambda m: (m, 0)),
                  pl.BlockSpec((1, B_W), lambda m: (0, 0)),
                  pl.BlockSpec((1, B_W), lambda m: (0, 0))],
        out_specs=[pl.BlockSpec((SEQ_TILE, B_W), lambda m: (m, 0)),
                   pl.BlockSpec((CHUNKS_PER_TILE, GATE_LANES, GDN_CHUNK), lambda m: (m, 0, 0))],
        out_shape=[jax.ShapeDtypeStruct((N_TOK, B_W), f32),
                   jax.ShapeDtypeStruct((N_TOK // GDN_CHUNK, GATE_LANES, GDN_CHUNK), f32)],
        compiler_params=_cparams(1),
        name="gdn_gate_prep",
    )(u_b, pad(a_log), pad(dt_bias))


def _gdn_conv_silu(x_ref, w_ref, t_len):
    x = x_ref[...].astype(f32)
    row = lax.broadcasted_iota(jnp.int32, (t_len, 1), 0)
    before = jnp.where(row == 0, 0.0, pltpu.roll(x, 1, 0))
    after = jnp.where(row == t_len - 1, 0.0, pltpu.roll(x, t_len - 1, 0))
    return _silu(before * w_ref[0:1, :] + x * w_ref[1:2, :] + after * w_ref[2:3, :])


def _l2norm_heads(x, n_heads):
    hd = GDN_HEAD_DIM
    parts = []
    for h in range(n_heads):
        xh = x[:, h * hd:(h + 1) * hd]
        parts.append(xh * lax.rsqrt(jnp.sum(xh * xh, axis=-1, keepdims=True) + EPS))
    return parts[0] if n_heads == 1 else jnp.concatenate(parts, axis=1)


GDN_STEP_ROWS = GDN_HEAD_DIM + GDN_CHUNK


def _gdn_prepare(grp, head0, q_s, k_s, v_s, gcol_ref, grow_ref, pq_s, b_s, gl_s, o_s, *, group, hp):
    cs, hd = GDN_CHUNK, GDN_HEAD_DIM
    rows = group * cs
    r0 = pl.multiple_of(grp * rows, rows)
    lane = lax.broadcasted_iota(jnp.int32, (1, B_W), 1)
    ri = lax.broadcasted_iota(jnp.int32, (cs, cs), 0)
    ci = lax.broadcasted_iota(jnp.int32, (cs, cs), 1)
    blk = ri ^ ci
    probs = []
    for j in range(group):
        rj = r0 + j * cs
        gates = gcol_ref[pl.ds(rj, cs), :]
        pick = lambda idx, gates=gates: jnp.sum(jnp.where(lane == idx, gates, 0.0), axis=1, keepdims=True)
        for hh in range(hp):
            cols = slice(hh * hd, (hh + 1) * hd)
            head = head0 + hh
            q = q_s[pl.ds(rj, cs), cols]
            k = k_s[pl.ds(rj, cs), cols]
            v = v_s[pl.ds(rj, cs), cols]
            kk = _dot_nt(k, k)
            qk = _dot_nt(q, k)
            for direction in range(2):
                beta = pick(direction * GDN_HEADS + head)
                dec_lane = (2 + direction) * GDN_HEADS + head
                gc = pick(dec_lane)
                gr = grow_ref[grp * group + j, pl.ds(dec_lane, 1), :]
                if direction == 0:
                    incl, strict = ri >= ci, ri > ci
                    g_tot = gc[cs - 1:cs, :]
                else:
                    incl, strict = ri <= ci, ri < ci
                    g_tot = gc[0:1, :]
                decay = jnp.exp(jnp.where(incl, gc - gr, -jnp.inf))
                probs.append(dict(j=j, hh=hh, d=direction, q=q, k=k, v=v, beta=beta, gc=gc, g_tot=g_tot,
                                  qk=qk * decay, lmat=jnp.where(strict, kk * decay * beta, 0.0)))
    for p in probs:
        p["n"] = -jnp.where(blk < 2, p["lmat"], 0.0)
    for level in range(1, int(math.log2(cs))):
        mask = (blk >= (1 << level)) & (blk < (2 << level))
        for p in probs:
            c_k = jnp.where(mask, p["lmat"], 0.0)
            y = c_k + _dot(c_k, p["n"])
            p["n"] = p["n"] - y - _dot(p["n"], y)
    for p in probs:
        j, hh, d = p["j"], p["hh"], p["d"]
        e_gc = jnp.exp(p["gc"])
        vb = p["v"] * p["beta"]
        kbg = p["k"] * (p["beta"] * e_gc)
        uw = _dot(p["n"], jnp.concatenate([vb, kbg], axis=1))
        wu = jnp.concatenate([kbg + uw[:, hd:], vb + uw[:, 0:hd]], axis=1)
        kd = p["k"] * jnp.exp(p["g_tot"] - p["gc"])
        top = _dot_tn(kd, wu)
        bot = _dot(p["qk"], wu)
        c = grp * group + j
        pq_s[hh, d, c, 0:hd, :] = top[:, 0:hd].astype(pq_s.dtype)
        pq_s[hh, d, c, hd:, :] = (p["q"] * e_gc - bot[:, 0:hd]).astype(pq_s.dtype)
        b_s[hh, d, c] = top[:, hd:]
        o_s[hh, d, pl.ds(r0 + j * cs, cs), :] = bot[:, hd:]
        gl_s[hh, d, c] = jnp.broadcast_to(jnp.exp(p["g_tot"]), (1, hd))


def _gdn_scan_step(c, hh, d, state, pq_s, b_s, gl_s, o_s):
    hd = GDN_HEAD_DIM
    r0 = pl.multiple_of(c * GDN_CHUNK, GDN_CHUNK)
    r = _dot(pq_s[hh, d, c], state)
    o_s[hh, d, pl.ds(r0, GDN_CHUNK), :] += r[hd:]
    return state * gl_s[hh, d, c] - r[0:hd] + b_s[hh, d, c]


def _gdn_kernel(q_ref, k_ref, v_ref, z_ref, wq_ref, wk_ref, wv_ref, gcol_ref, grow_ref, ng_ref, s0_ref,
                y_ref, sfin_ref, q_s, k_s, v_s, o_s, pq_s, b_s, gl_s, *, t_len, group, hp):
    hd = GDN_HEAD_DIM
    head0 = pl.program_id(1) * hp
    n_chunks = t_len // GDN_CHUNK
    q_s[...] = _l2norm_heads(_gdn_conv_silu(q_ref, wq_ref, t_len), hp) * (hd ** -0.5)
    k_s[...] = _l2norm_heads(_gdn_conv_silu(k_ref, wk_ref, t_len), hp)
    v_s[...] = _gdn_conv_silu(v_ref, wv_ref, t_len)

    def prepare(grp, carry):
        _gdn_prepare(grp, head0, q_s, k_s, v_s, gcol_ref, grow_ref, pq_s, b_s, gl_s, o_s, group=group, hp=hp)
        return carry

    lax.fori_loop(0, n_chunks // group, prepare, 0)

    def scan(i, states):
        new = []
        for hh in range(hp):
            new.append(_gdn_scan_step(i, hh, 0, states[2 * hh], pq_s, b_s, gl_s, o_s))
            new.append(_gdn_scan_step(n_chunks - 1 - i, hh, 1, states[2 * hh + 1], pq_s, b_s, gl_s, o_s))
        return tuple(new)

    init = tuple(s0_ref[d, hh] for hh in range(hp) for d in range(2))
    final = lax.fori_loop(0, n_chunks, scan, init)
    outs = []
    for hh in range(hp):
        sfin_ref[0, hh] = final[2 * hh]
        sfin_ref[1, hh] = final[2 * hh + 1]
        o = o_s[hh, 0] + o_s[hh, 1]
        outs.append(o * lax.rsqrt(jnp.mean(o * o, axis=-1, keepdims=True) + EPS) * ng_ref[...])
    o = outs[0] if hp == 1 else jnp.concatenate(outs, axis=1)
    y_ref[...] = (o * _silu(z_ref[...].astype(f32))).astype(y_ref.dtype)


def _gdn(u_a, gcols, grows, conv_w, norm_g, s0, t_len, n_seq, row_off, hp, group):
    hd = GDN_HEAD_DIM
    wd = hd * hp
    nb = GDN_HEADS // hp
    blk0 = row_off // t_len
    c_q = POOL_W // wd
    col = lambda base: pl.BlockSpec((t_len, wd), lambda s, h: (blk0 + s, base + h))
    cw = lambda base: pl.BlockSpec((3, wd), lambda s, h: (0, base + h))
    n_chunks = t_len // GDN_CHUNK
    state_spec = pl.BlockSpec((None, 2, hp, hd, hd), lambda s, h: (s, 0, h, 0, 0))
    return pl.pallas_call(
        functools.partial(_gdn_kernel, t_len=t_len, group=group, hp=hp),
        grid=(n_seq, nb),
        in_specs=[col(c_q), col(c_q + nb), col(c_q + 2 * nb), col(c_q + 3 * nb),
                  cw(0), cw(nb), cw(2 * nb),
                  pl.BlockSpec((t_len, B_W), lambda s, h: (blk0 + s, 0)),
                  pl.BlockSpec((n_chunks, GATE_LANES, GDN_CHUNK), lambda s, h: (blk0 + s, 0, 0)),
                  pl.BlockSpec((1, hd), lambda s, h: (0, 0)),
                  state_spec],
        out_specs=[pl.BlockSpec((t_len, wd), lambda s, h: (s, h)), state_spec],
        out_shape=[jax.ShapeDtypeStruct((n_seq * t_len, GDN_W), bf16),
                   jax.ShapeDtypeStruct((n_seq, 2, GDN_HEADS, hd, hd), f32)],
        scratch_shapes=[pltpu.VMEM((t_len, wd), f32)] * 3
        + [pltpu.VMEM((hp, 2, t_len, hd), f32),
           pltpu.VMEM((hp, 2, n_chunks, GDN_STEP_ROWS, hd), bf16),
           pltpu.VMEM((hp, 2, n_chunks, hd, hd), f32),
           pltpu.VMEM((hp, 2, n_chunks, 1, hd), f32)],
        compiler_params=_cparams(2),
        name=f"gdn_t{t_len}",
    )(u_a, u_a, u_a, u_a, conv_w, conv_w, conv_w, gcols, grows, norm_g.reshape(1, hd), s0)


def _ctx_attn_kernel(x_ref, o_ref):
    hd = NA_HEAD_DIM
    outs = []
    for h in range(NA_HEADS):
        q = x_ref[:, h * hd:(h + 1) * hd]
        k = x_ref[:, NA_W + h * hd:NA_W + (h + 1) * hd]
        v = x_ref[:, 2 * NA_W + h * hd:2 * NA_W + (h + 1) * hd]
        s = _dot_nt(q, k) * (hd ** -0.5)
        p = jnp.exp(s - jnp.max(s, axis=-1, keepdims=True))
        outs.append(_dot(p, v) / jnp.sum(p, axis=-1, keepdims=True))
    o_ref[...] = jnp.concatenate(outs, axis=1).astype(o_ref.dtype)


def _ctx_attention(u_c):
    return pl.pallas_call(
        _ctx_attn_kernel,
        grid=(BATCH,),
        in_specs=[pl.BlockSpec((SEQ, C_W), lambda b: (b, 0))],
        out_specs=pl.BlockSpec((SEQ, NA_W), lambda b: (b, 0)),
        out_shape=jax.ShapeDtypeStruct((N_CTX, NA_W), bf16),
        compiler_params=_cparams(1),
        name="ctx_attention",
    )(u_c)


HEADS_PER_STEP = 128 // NA_HEAD_DIM
NB_KEYS = NA_WR * GRID_W


def _na_kernel(q_ref, k_ref, v_ref, kc_ref, vc_ref, bias_ref, o_ref, k_s, v_s, kc_s, vc_s):
    hd = NA_HEAD_DIM
    scale = hd ** -0.5
    k_s[...] = k_ref[...].astype(bf16)
    v_s[...] = v_ref[...].astype(bf16)
    kc_s[...] = kc_ref[...].astype(bf16)
    vc_s[...] = vc_ref[...].astype(bf16)
    lane = lax.broadcasted_iota(jnp.int32, (1, HEADS_PER_STEP * hd), 1)

    def body(r, carry):
        start = jnp.clip(r - NA_WR // 2, 0, GRID_ROWS - NA_WR)
        case = start - r + NA_WR - 1
        q0 = pl.multiple_of(r * GRID_W, GRID_W)
        k0 = pl.multiple_of(start * GRID_W, GRID_W)
        q = q_ref[pl.ds(q0, GRID_W), :] * scale
        k_nb = k_s[pl.ds(k0, NB_KEYS), :]
        v_nb = v_s[pl.ds(k0, NB_KEYS), :]
        out = None
        for hh in range(HEADS_PER_STEP):
            own = (lane >= hh * hd) & (lane < (hh + 1) * hd)
            qh = jnp.where(own, q, 0.0).astype(bf16)
            s_nb = _dot_nt(qh, k_nb) + bias_ref[hh, case]
            s_cx = _dot_nt(qh, kc_s[...])
            mx = jnp.maximum(jnp.max(s_nb, axis=-1, keepdims=True), jnp.max(s_cx, axis=-1, keepdims=True))
            p_nb = jnp.exp(s_nb - mx)
            p_cx = jnp.exp(s_cx - mx)
            den = jnp.sum(p_nb, axis=-1, keepdims=True) + jnp.sum(p_cx, axis=-1, keepdims=True)
            o = (_dot(p_nb, v_nb) + _dot(p_cx, vc_s[...])) / den
            out = o if out is None else jnp.where(own, o, out)
        o_ref[pl.ds(q0, GRID_W), :] = out.astype(o_ref.dtype)
        return carry

    lax.fori_loop(0, GRID_ROWS, body, 0, unroll=2)


def _na_bias_tiles(rpb):
    col = np.arange(GRID_W)
    col_start = np.clip(col - NA_WC // 2, 0, GRID_W - NA_WC)
    col_mask = (col[None, :] >= col_start[:, None]) & (col[None, :] < col_start[:, None] + NA_WC)
    d_col = np.clip(col[None, :] - col[:, None], -(NA_WC - 1), NA_WC - 1) + NA_WC - 1
    onehot = (d_col[:, :, None] == np.arange(2 * NA_WC - 1)).astype(np.float32)
    picked = jnp.einsum("hrd,qkd->hrqk", rpb, onehot, precision=lax.Precision.HIGHEST)
    table = jnp.where(col_mask[None, None], picked, NEG_BIG)
    tiles = [table[:, case:case + NA_WR].transpose(0, 2, 1, 3).reshape(NA_HEADS, GRID_W, NB_KEYS)
             for case in range(NA_WR)]
    return jnp.stack(tiles, axis=1).astype(f32)


def _na_attention(u_c, k_ctx, v_ctx, bias_tiles):
    blk0 = N_CTX // DEC_SEQ
    nhp = NA_HEADS // HEADS_PER_STEP
    col = lambda base: pl.BlockSpec((DEC_SEQ, 128), lambda b, p: (blk0 + b, base + p))
    cache = pl.BlockSpec((None, PAST_LEN, 128), lambda b, p: (b, 0, p))
    return pl.pallas_call(
        _na_kernel,
        grid=(DEC_BATCH, nhp),
        in_specs=[col(0), col(nhp), col(2 * nhp), cache, cache,
                  pl.BlockSpec((HEADS_PER_STEP, NA_WR, GRID_W, NB_KEYS), lambda b, p: (p, 0, 0, 0))],
        out_specs=pl.BlockSpec((DEC_SEQ, 128), lambda b, p: (b, p)),
        out_shape=jax.ShapeDtypeStruct((N_LAT, NA_W), bf16),
        scratch_shapes=[pltpu.VMEM((DEC_SEQ, 128), bf16)] * 2 + [pltpu.VMEM((PAST_LEN, 128), bf16)] * 2,
        compiler_params=_cparams(2),
        name="na_attention",
    )(u_c, u_c, u_c, k_ctx, v_ctx, bias_tiles)


def _final_norm_kernel(x_ref, g_ref, o_ref):
    x = x_ref[...]
    o_ref[...] = x * lax.rsqrt(jnp.mean(x * x, axis=-1, keepdims=True) + EPS) * g_ref[...]


def _final_norm(x, g, n_rows, row_off):
    tm = 512
    blk0 = row_off // tm
    return pl.pallas_call(
        _final_norm_kernel,
        grid=(n_rows // tm,),
        in_specs=[pl.BlockSpec((tm, D_MODEL), lambda m: (blk0 + m, 0)),
                  pl.BlockSpec((1, D_MODEL), lambda m: (0, 0))],
        out_specs=pl.BlockSpec((tm, D_MODEL), lambda m: (m, 0)),
        out_shape=jax.ShapeDtypeStruct((n_rows, D_MODEL), f32),
        compiler_params=_cparams(1),
        name="final_norm",
    )(x, g.reshape(1, D_MODEL))


def kernel(x_prompt, x_sample, cache_na_k, cache_na_v, state_gdn, c, c_ctx, w_ada, b_ada, g_norm1, w_in, pool_w, pool_scale, gdn_conv, gdn_a_log, gdn_dt_bias, gdn_norm_g, na_rpb, w_branch_pool, w_branch_gdn, w_branch_na, w_out, g_norm2, w_up, ffn_conv, w_down, g_final):
    x = jnp.concatenate([x_prompt.reshape(N_CTX, D_MODEL), x_sample.reshape(N_LAT, D_MODEL)], axis=0)

    cvec = jnp.zeros((8, D_MODEL), f32).at[0].set(c_ctx).at[1:1 + DEC_BATCH].set(c)
    mods = _modulation(cvec, w_ada, b_ada)
    group_row = np.array([0] * (N_CTX // ROW_GROUP) + list(range(1, 1 + DEC_BATCH)))
    mods = mods[:, group_row].reshape(DEPTH, N_GROUPS, 6, 1, D_MODEL).transpose(0, 2, 1, 3, 4)

    gates_end = A_W + GATE_LANES
    w_in_r = jnp.concatenate(
        [w_in[:, :, :gates_end], jnp.zeros((DEPTH, D_MODEL, IN_PAD_B - GATE_LANES), f32), w_in[:, :, gates_end:]],
        axis=-1).astype(bf16)

    zero_state = jnp.zeros((BATCH, 2, GDN_HEADS, GDN_HEAD_DIM, GDN_HEAD_DIM), f32)
    new_k, new_v, new_s = [], [], []
    for l in range(DEPTH):
        sh1, sc1, g1, sh2, sc2, g2 = (mods[l, j] for j in range(6))
        h = _norm_mod(x, g_norm1[l], sc1, sh1)
        u_a = _matmul(h, w_in_r[l], bf16, "in_proj_mix", 0, A_W)
        u_b = _matmul(h, w_in_r[l], f32, "in_proj_gates", IN_OFF_B, B_W)
        u_c = _matmul(h, w_in_r[l], f32, "in_proj_attn", IN_OFF_C, C_W)
        u_g = _matmul(h, w_in_r[l], bf16, "in_proj_branch_gates", IN_OFF_G, G_W)

        y_pool = _pool(u_a, pool_w[l].astype(bf16), pool_scale[l])

        gcols, grows = _gate_prep(u_b, gdn_a_log[l], gdn_dt_bias[l])
        y_gdn_ctx, s_ctx = _gdn(u_a, gcols, grows, gdn_conv[l], gdn_norm_g[l], zero_state, SEQ, BATCH, 0,
                                hp=4, group=SEQ // GDN_CHUNK)
        y_gdn_lat, _ = _gdn(u_a, gcols, grows, gdn_conv[l], gdn_norm_g[l], state_gdn[:, l], DEC_SEQ, DEC_BATCH,
                            N_CTX, hp=2, group=8)

        y_na_ctx = _ctx_attention(u_c)
        y_na_lat = _na_attention(u_c, cache_na_k[:, l].reshape(DEC_BATCH, PAST_LEN, NA_W),
                                 cache_na_v[:, l].reshape(DEC_BATCH, PAST_LEN, NA_W), _na_bias_tiles(na_rpb[l]))

        merged = _merge(y_pool, y_gdn_ctx, y_gdn_lat, y_na_ctx, y_na_lat, w_branch_pool[l].astype(bf16),
                        w_branch_gdn[l].astype(bf16), w_branch_na[l].astype(bf16), u_g)
        x = _matmul_residual(merged, w_out[l].astype(bf16), x, g1, "out_proj")

        h = _norm_mod(x, g_norm2[l], sc2, sh2)
        act = _ffn_up_glu(h, w_up[l], ffn_conv[l])
        x = _matmul_residual(act, w_down[l].astype(bf16), x, g2, "ffn_down")

        new_k.append(u_c[:N_CTX, NA_W:2 * NA_W].reshape(BATCH, SEQ, NA_HEADS, NA_HEAD_DIM))
        new_v.append(u_c[:N_CTX, 2 * NA_W:].reshape(BATCH, SEQ, NA_HEADS, NA_HEAD_DIM))
        new_s.append(s_ctx)

    y_prompt = _final_norm(x, g_final, N_CTX, 0).reshape(BATCH, SEQ, D_MODEL)
    y_sample = _final_norm(x, g_final, N_LAT, N_CTX).reshape(DEC_BATCH, DEC_SEQ, D_MODEL)
    return (y_prompt, y_sample, jnp.stack(new_k, axis=1), jnp.stack(new_v, axis=1), jnp.stack(new_s, axis=1))
```

```python
import functools
import math

import numpy as np
import jax
import jax.numpy as jnp
from jax import lax
from jax.experimental import pallas as pl
from jax.experimental.pallas import tpu as pltpu

D_MODEL = 2048
BATCH = 16
SEQ = 256
DEPTH = 2
DEC_BATCH = 2
DEC_SEQ = 2048
PAST_LEN = 512
GRID_W = 64
POOL_GROUPS = 4
POOL_GC = 128
POOL_W = POOL_GROUPS * POOL_GC
POOL_WINDOWS = (2, 4, 8, 16)
GDN_HEADS = 8
GDN_HEAD_DIM = 128
GDN_W = GDN_HEADS * GDN_HEAD_DIM
GDN_CHUNK = 64
NA_HEADS = 8
NA_HEAD_DIM = 64
NA_W = NA_HEADS * NA_HEAD_DIM
NA_WR = 8
NA_WC = 16
D_FF = 5632
EPS = 1e-6

N_CTX = BATCH * SEQ
N_LAT = DEC_BATCH * DEC_SEQ
N_TOK = N_CTX + N_LAT
ROW_GROUP = DEC_SEQ
N_GROUPS = N_TOK // ROW_GROUP
SEQ_TILE = SEQ
HALO = 16
GRID_ROWS = DEC_SEQ // GRID_W
NEG_BIG = -1e30

A_W = POOL_W + 4 * GDN_W
B_W = 128
C_W = 3 * NA_W
G_W = 3 * D_MODEL

VMEM_LIMIT = 48 * 1024 * 1024

f32 = jnp.float32
bf16 = jnp.bfloat16


def _cparams(n_axes):
    return pltpu.CompilerParams(dimension_semantics=("arbitrary",) * n_axes, vmem_limit_bytes=VMEM_LIMIT)


def _dot(a, b):
    return jnp.dot(a.astype(bf16), b.astype(bf16), preferred_element_type=f32)


def _dot_nt(a, b):
    return lax.dot_general(a.astype(bf16), b.astype(bf16), (((1,), (1,)), ((), ())), preferred_element_type=f32)


def _dot_tn(a, b):
    return lax.dot_general(a.astype(bf16), b.astype(bf16), (((0,), (0,)), ((), ())), preferred_element_type=f32)


def _silu(x):
    return x * jax.nn.sigmoid(x)


def _col_tile(n, cap=1536):
    best = 128
    for t in range(128, cap + 1, 128):
        if n % t == 0:
            best = t
    return best


def _mod_kernel(c_ref, w_ref, b_ref, o_ref):
    a = _silu(c_ref[...])
    o_ref[...] = _dot(a, w_ref[...]) + b_ref[...]


def _modulation(cvec, w_ada, b_ada):
    tn = 1024
    n = 6 * D_MODEL
    return pl.pallas_call(
        _mod_kernel,
        grid=(DEPTH, n // tn),
        in_specs=[
            pl.BlockSpec((8, D_MODEL), lambda l, j: (0, 0)),
            pl.BlockSpec((None, D_MODEL, tn), lambda l, j: (l, 0, j)),
            pl.BlockSpec((None, 1, tn), lambda l, j: (l, 0, j)),
        ],
        out_specs=pl.BlockSpec((None, 8, tn), lambda l, j: (l, 0, j)),
        out_shape=jax.ShapeDtypeStruct((DEPTH, 8, n), f32),
        compiler_params=_cparams(2),
        name="modulation",
    )(cvec, w_ada, b_ada.reshape(DEPTH, 1, n))


def _mod_spec(tm, tn, col_of):
    return pl.BlockSpec((None, 1, tn), lambda m, n: ((m * tm) // ROW_GROUP, 0, col_of(n)))


def _norm_mod_kernel(x_ref, g_ref, sc_ref, sh_ref, o_ref):
    x = x_ref[...]
    y = x * lax.rsqrt(jnp.mean(x * x, axis=-1, keepdims=True) + EPS) * g_ref[...]
    o_ref[...] = (y * (1.0 + sc_ref[...]) + sh_ref[...]).astype(o_ref.dtype)


def _norm_mod(x, g, scale, shift):
    tm = 512
    mod = pl.BlockSpec((None, 1, D_MODEL), lambda m: ((m * tm) // ROW_GROUP, 0, 0))
    return pl.pallas_call(
        _norm_mod_kernel,
        grid=(N_TOK // tm,),
        in_specs=[
            pl.BlockSpec((tm, D_MODEL), lambda m: (m, 0)),
            pl.BlockSpec((1, D_MODEL), lambda m: (0, 0)),
            mod, mod,
        ],
        out_specs=pl.BlockSpec((tm, D_MODEL), lambda m: (m, 0)),
        out_shape=jax.ShapeDtypeStruct((N_TOK, D_MODEL), bf16),
        compiler_params=_cparams(1),
        name="norm_mod",
    )(x, g.reshape(1, D_MODEL), scale, shift)


LANE = 128


def _in_proj_kernel(x_ref, w_ref, wn_ref, o_ref, w_s, *, shift, n_valid):
    @pl.when(pl.program_id(1) == 0)
    def _():
        w = w_ref[...]
        if shift:
            w = jnp.concatenate([w[:, shift:], wn_ref[:, 0:shift]], axis=1)
        if n_valid < w.shape[1]:
            lane = lax.broadcasted_iota(jnp.int32, (1, w.shape[1]), 1)
            w = jnp.where(lane < n_valid, w, 0.0)
        w_s[...] = w.astype(bf16)

    o_ref[...] = jnp.dot(x_ref[...], w_s[...], preferred_element_type=f32).astype(o_ref.dtype)


def _in_proj(x, w_all, layer, col0, n, out_dtype, name):
    m, k = x.shape
    shift = col0 % LANE
    base = col0 - shift
    n_out = -(-n // LANE) * LANE
    tm = 1024
    tn = _col_tile(math.gcd(n_out, base) if base else n_out, 1024)
    j0 = base // tn
    per = tn // LANE
    return pl.pallas_call(
        functools.partial(_in_proj_kernel, shift=shift, n_valid=n if n < n_out else n_out),
        grid=(n_out // tn, m // tm),
        in_specs=[
            pl.BlockSpec((tm, k), lambda j, i: (i, 0)),
            pl.BlockSpec((None, k, tn), lambda j, i: (layer, 0, j0 + j)),
            pl.BlockSpec((None, k, LANE), lambda j, i: (layer, 0, (j0 + j + 1) * per)),
        ],
        out_specs=pl.BlockSpec((tm, tn), lambda j, i: (i, j)),
        out_shape=jax.ShapeDtypeStruct((m, n_out), out_dtype),
        scratch_shapes=[pltpu.VMEM((k, tn), bf16)],
        compiler_params=_cparams(2),
        name=name,
    )(x, w_all, w_all)


def _mm_res_kernel(a_ref, w_ref, x_ref, g_ref, o_ref):
    y = jnp.dot(a_ref[...], w_ref[...], preferred_element_type=f32)
    o_ref[...] = x_ref[...] + g_ref[...] * y


def _matmul_residual(a, w, x, gate, name):
    m, k = a.shape
    n = w.shape[1]
    tm = 1024
    tn = 1024 if k <= D_MODEL else 512
    return pl.pallas_call(
        _mm_res_kernel,
        grid=(m // tm, n // tn),
        in_specs=[
            pl.BlockSpec((tm, k), lambda i, j: (i, 0)),
            pl.BlockSpec((k, tn), lambda i, j: (0, j)),
            pl.BlockSpec((tm, tn), lambda i, j: (i, j)),
            _mod_spec(tm, tn, lambda j: j),
        ],
        out_specs=pl.BlockSpec((tm, tn), lambda i, j: (i, j)),
        out_shape=jax.ShapeDtypeStruct((m, n), f32),
        compiler_params=_cparams(2),
        name=name,
    )(a, w, x, gate)


MERGE_TM = 1024


def _merge_kernel(yp_ref, ygc_ref, ygl_ref, ync_ref, ynl_ref, wp_ref, wg_ref, wn_ref, g0_ref, g1_ref, g2_ref, o_ref):
    is_ctx = pl.program_id(0) < N_CTX // MERGE_TM

    def branch(y, w_ref, g_ref):
        return jax.nn.sigmoid(g_ref[...].astype(f32)) * jnp.dot(y, w_ref[...], preferred_element_type=f32)

    y_gdn = jnp.where(is_ctx, ygc_ref[...], ygl_ref[...])
    y_na = jnp.where(is_ctx, ync_ref[...], ynl_ref[...])
    acc = branch(yp_ref[...], wp_ref, g0_ref) + branch(y_gdn, wg_ref, g1_ref) + branch(y_na, wn_ref, g2_ref)
    o_ref[...] = acc.astype(o_ref.dtype)


def _merge(y_pool, y_gdn_ctx, y_gdn_lat, y_na_ctx, y_na_lat, w_pool, w_gdn, w_na, u_gate):
    tm = MERGE_TM
    tn = 1024
    nb = D_MODEL // tn
    n_ctx = N_CTX // tm
    row = lambda width: pl.BlockSpec((tm, width), lambda i, j: (i, 0))
    ctx = lambda width: pl.BlockSpec((tm, width), lambda i, j: (jnp.minimum(i, n_ctx - 1), 0))
    lat = lambda width: pl.BlockSpec((tm, width), lambda i, j: (jnp.maximum(i - n_ctx, 0), 0))
    wcol = lambda kdim: pl.BlockSpec((kdim, tn), lambda i, j: (0, j))
    gate = lambda b: pl.BlockSpec((tm, tn), lambda i, j: (i, j + b * nb))
    return pl.pallas_call(
        _merge_kernel,
        grid=(N_TOK // tm, nb),
        in_specs=[row(POOL_W), ctx(GDN_W), lat(GDN_W), ctx(NA_W), lat(NA_W),
                  wcol(POOL_W), wcol(GDN_W), wcol(NA_W), gate(0), gate(1), gate(2)],
        out_specs=pl.BlockSpec((tm, tn), lambda i, j: (i, j)),
        out_shape=jax.ShapeDtypeStruct((N_TOK, D_MODEL), bf16),
        compiler_params=_cparams(2),
        name="merge",
    )(y_pool, y_gdn_ctx, y_gdn_lat, y_na_ctx, y_na_lat, w_pool, w_gdn, w_na, u_gate, u_gate, u_gate)


def _tile_seq_len(m):
    return jnp.where(m * SEQ_TILE < N_CTX, SEQ, DEC_SEQ)


def _halo_specs(width, col_of):
    per = SEQ_TILE // HALO
    last = N_TOK // HALO - 1
    main = pl.BlockSpec((SEQ_TILE, width), lambda m, c: (m, col_of(c)))
    prev = pl.BlockSpec((HALO, width), lambda m, c: (jnp.maximum(m * per - 1, 0), col_of(c)))
    nxt = pl.BlockSpec((HALO, width), lambda m, c: (jnp.minimum((m + 1) * per, last), col_of(c)))
    return main, prev, nxt


def _tile_pos(m):
    t_len = _tile_seq_len(m)
    row = lax.broadcasted_iota(jnp.int32, (SEQ_TILE, 1), 0)
    return (m * SEQ_TILE + row) & (t_len - 1), t_len


def _pool_kernel(x_ref, xp_ref, xn_ref, w_ref, s_ref, o_ref, ext_ref):
    m = pl.program_id(0)
    pos, t_len = _tile_pos(m)
    first = ((m * SEQ_TILE) & (t_len - 1)) == 0
    last = (((m + 1) * SEQ_TILE) & (t_len - 1)) == 0
    x = x_ref[...].astype(f32)
    ext_ref[0:HALO, :] = jnp.where(first, 0.0, xp_ref[...].astype(f32))
    ext_ref[HALO:HALO + SEQ_TILE, :] = x
    ext_ref[HALO + SEQ_TILE:, :] = jnp.where(last, 0.0, xn_ref[...].astype(f32))
    outs = []
    for g, win in enumerate(POOL_WINDOWS):
        cols = slice(g * POOL_GC, (g + 1) * POOL_GC)
        acc = jnp.zeros((SEQ_TILE, POOL_GC), f32)
        for s in range(-(win // 2), win - win // 2):
            acc = acc + ext_ref[HALO + s:HALO + s + SEQ_TILE, cols]
        lo = jnp.maximum(pos - win // 2, 0)
        hi = jnp.minimum(pos + win - 1 - win // 2, t_len - 1)
        mean = acc / (hi - lo + 1).astype(f32)
        y = _dot(mean - x[:, cols], w_ref[g]) * s_ref[:, cols]
        outs.append(y)
    o_ref[...] = jnp.concatenate(outs, axis=1).astype(o_ref.dtype)


def _pool(u_a, pool_w, pool_scale):
    main, prev, nxt = _halo_specs(POOL_W, lambda c: 0)
    return pl.pallas_call(
        _pool_kernel,
        grid=(N_TOK // SEQ_TILE, 1),
        in_specs=[main, prev, nxt,
                  pl.BlockSpec((POOL_GROUPS, POOL_GC, POOL_GC), lambda m, c: (0, 0, 0)),
                  pl.BlockSpec((1, POOL_W), lambda m, c: (0, 0))],
        out_specs=pl.BlockSpec((SEQ_TILE, POOL_W), lambda m, c: (m, 0)),
        out_shape=jax.ShapeDtypeStruct((N_TOK, POOL_W), bf16),
        scratch_shapes=[pltpu.VMEM((SEQ_TILE + 2 * HALO, POOL_W), f32)],
        compiler_params=_cparams(2),
        name="pool",
    )(u_a, u_a, u_a, pool_w, pool_scale.reshape(1, POOL_W))


FFN_TM = 1024
FFN_TC = 512
FFN_SUB = 128


def _ffn_up_kernel(x_ref, xp_ref, xn_ref, wa_ref, wb_ref, ca_ref, cb_ref, o_ref, w_s):
    i = pl.program_id(1)
    n_sub = FFN_TC // FFN_SUB
    sub = FFN_SUB

    @pl.when(i == 0)
    def _():
        for s in range(n_sub):
            w_s[:, 2 * s * sub:(2 * s + 1) * sub] = wa_ref[:, s * sub:(s + 1) * sub].astype(bf16)
            w_s[:, (2 * s + 1) * sub:(2 * s + 2) * sub] = wb_ref[:, s * sub:(s + 1) * sub].astype(bf16)

    rows = FFN_TM + 2 * HALO
    x_ext = jnp.concatenate([xp_ref[...], x_ref[...], xn_ref[...]], axis=0)
    t_len = jnp.where(i * FFN_TM < N_CTX, SEQ, DEC_SEQ)
    pos = (i * FFN_TM + lax.broadcasted_iota(jnp.int32, (FFN_TM, 1), 0)) & (t_len - 1)
    first = pos == 0
    last = pos == t_len - 1
    for s in range(n_sub):
        ab = jnp.dot(x_ext, w_s[:, 2 * s * sub:(2 * s + 2) * sub], preferred_element_type=f32)
        taps = jnp.concatenate([ca_ref[:, s * sub:(s + 1) * sub], cb_ref[:, s * sub:(s + 1) * sub]], axis=1)
        before = jnp.where(first, 0.0, pltpu.roll(ab, 1, 0)[HALO:HALO + FFN_TM])
        after = jnp.where(last, 0.0, pltpu.roll(ab, rows - 1, 0)[HALO:HALO + FFN_TM])
        y = before * taps[0:1] + ab[HALO:HALO + FFN_TM] * taps[1:2] + after * taps[2:3]
        o_ref[:, s * sub:(s + 1) * sub] = (_silu(y[:, 0:sub]) * y[:, sub:]).astype(o_ref.dtype)


def _ffn_up_glu(h, w_up_all, layer, ffn_conv):
    nb = D_FF // FFN_TC
    per = FFN_TM // HALO
    last = N_TOK // HALO - 1
    return pl.pallas_call(
        _ffn_up_kernel,
        grid=(nb, N_TOK // FFN_TM),
        in_specs=[pl.BlockSpec((FFN_TM, D_MODEL), lambda j, i: (i, 0)),
                  pl.BlockSpec((HALO, D_MODEL), lambda j, i: (jnp.maximum(i * per - 1, 0), 0)),
                  pl.BlockSpec((HALO, D_MODEL), lambda j, i: (jnp.minimum((i + 1) * per, last), 0)),
                  pl.BlockSpec((None, D_MODEL, FFN_TC), lambda j, i: (layer, 0, j)),
                  pl.BlockSpec((None, D_MODEL, FFN_TC), lambda j, i: (layer, 0, j + nb)),
                  pl.BlockSpec((3, FFN_TC), lambda j, i: (0, j)),
                  pl.BlockSpec((3, FFN_TC), lambda j, i: (0, j + nb))],
        out_specs=pl.BlockSpec((FFN_TM, FFN_TC), lambda j, i: (i, j)),
        out_shape=jax.ShapeDtypeStruct((N_TOK, D_FF), bf16),
        scratch_shapes=[pltpu.VMEM((D_MODEL, 2 * FFN_TC), bf16)],
        compiler_params=_cparams(2),
        name="ffn_up_glu",
    )(h, h, h, w_up_all, w_up_all, ffn_conv, ffn_conv)


GATE_LANES = 4 * GDN_HEADS
CHUNKS_PER_TILE = SEQ_TILE // GDN_CHUNK


def _gate_prep_kernel(u_ref, alog_ref, dtb_ref, cols_ref, rows_ref):
    u = u_ref[...]
    lane = lax.broadcasted_iota(jnp.int32, (1, B_W), 1)
    x = u + dtb_ref[...]
    softplus = jnp.maximum(x, 0.0) + jnp.log1p(jnp.exp(-jnp.abs(x)))
    g = -jnp.exp(alog_ref[...]) * softplus
    r = lax.broadcasted_iota(jnp.int32, (SEQ_TILE, SEQ_TILE), 0)
    c = lax.broadcasted_iota(jnp.int32, (SEQ_TILE, SEQ_TILE), 1)
    shift = int(math.log2(GDN_CHUNK))
    same = (r >> shift) == (c >> shift)
    tri_f = jnp.where(same & (c <= r), 1.0, 0.0)
    tri_b = jnp.where(same & (c >= r), 1.0, 0.0)
    gc_f = jnp.dot(tri_f, g, preferred_element_type=f32, precision=lax.Precision.HIGHEST)
    gc_b = jnp.dot(tri_b, g, preferred_element_type=f32, precision=lax.Precision.HIGHEST)
    cols = jnp.where(lane < 2 * GDN_HEADS, jax.nn.sigmoid(u), jnp.where(lane < 3 * GDN_HEADS, gc_f, gc_b))
    cols_ref[...] = cols
    for j in range(CHUNKS_PER_TILE):
        t = cols[j * GDN_CHUNK:(j + 1) * GDN_CHUNK, :].T
        rows_ref[j] = t[0:GATE_LANES, :]


def _gate_prep(u_b, a_log, dt_bias):
    pad = lambda p: jnp.zeros((1, B_W), f32).at[0, 2 * GDN_HEADS:GATE_LANES].set(p.reshape(-1))
    n_tiles = N_TOK // SEQ_TILE
    return pl.pallas_call(
        _gate_prep_kernel,
        grid=(n_tiles,),
        in_specs=[pl.BlockSpec((SEQ_TILE, B_W), lambda m: (m, 0)),
                  pl.BlockSpec((1, B_W), lambda m: (0, 0)),
                  pl.BlockSpec((1, B_W), lambda m: (0, 0))],
        out_specs=[pl.BlockSpec((SEQ_TILE, B_W), lambda m: (m, 0)),
                   pl.BlockSpec((CHUNKS_PER_TILE, GATE_LANES, GDN_CHUNK), lambda m: (m, 0, 0))],
        out_shape=[jax.ShapeDtypeStruct((N_TOK, B_W), f32),
                   jax.ShapeDtypeStruct((N_TOK // GDN_CHUNK, GATE_LANES, GDN_CHUNK), f32)],
        compiler_params=_cparams(1),
        name="gdn_gate_prep",
    )(u_b, pad(a_log), pad(dt_bias))


def _gdn_conv_silu(x_ref, w_ref, t_len):
    x = x_ref[...].astype(f32)
    row = lax.broadcasted_iota(jnp.int32, (t_len, 1), 0)
    before = jnp.where(row == 0, 0.0, pltpu.roll(x, 1, 0))
    after = jnp.where(row == t_len - 1, 0.0, pltpu.roll(x, t_len - 1, 0))
    return _silu(before * w_ref[0:1, :] + x * w_ref[1:2, :] + after * w_ref[2:3, :])


def _l2norm_heads(x, n_heads):
    hd = GDN_HEAD_DIM
    parts = []
    for h in range(n_heads):
        xh = x[:, h * hd:(h + 1) * hd]
        parts.append(xh * lax.rsqrt(jnp.sum(xh * xh, axis=-1, keepdims=True) + EPS))
    return parts[0] if n_heads == 1 else jnp.concatenate(parts, axis=1)


GDN_STEP_ROWS = GDN_HEAD_DIM + GDN_CHUNK


def _gdn_prepare(grp, head0, q_s, k_s, v_s, gcol_ref, grow_ref, pq_s, b_s, gl_s, o_s, *, group, hp):
    cs, hd = GDN_CHUNK, GDN_HEAD_DIM
    rows = group * cs
    r0 = pl.multiple_of(grp * rows, rows)
    lane = lax.broadcasted_iota(jnp.int32, (1, B_W), 1)
    ri = lax.broadcasted_iota(jnp.int32, (cs, cs), 0)
    ci = lax.broadcasted_iota(jnp.int32, (cs, cs), 1)
    blk = ri ^ ci
    probs = []
    for j in range(group):
        rj = r0 + j * cs
        gates = gcol_ref[pl.ds(rj, cs), :]
        pick = lambda idx, gates=gates: jnp.sum(jnp.where(lane == idx, gates, 0.0), axis=1, keepdims=True)
        for hh in range(hp):
            cols = slice(hh * hd, (hh + 1) * hd)
            head = head0 + hh
            q = q_s[pl.ds(rj, cs), cols]
            k = k_s[pl.ds(rj, cs), cols]
            v = v_s[pl.ds(rj, cs), cols]
            kk = _dot_nt(k, k)
            qk = _dot_nt(q, k)
            for direction in range(2):
                beta = pick(direction * GDN_HEADS + head)
                dec_lane = (2 + direction) * GDN_HEADS + head
                gc = pick(dec_lane)
                gr = grow_ref[grp * group + j, pl.ds(dec_lane, 1), :]
                if direction == 0:
                    incl, strict = ri >= ci, ri > ci
                    g_tot = gc[cs - 1:cs, :]
                else:
                    incl, strict = ri <= ci, ri < ci
                    g_tot = gc[0:1, :]
                decay = jnp.exp(jnp.where(incl, gc - gr, -jnp.inf))
                probs.append(dict(j=j, hh=hh, d=direction, q=q, k=k, v=v, beta=beta, gc=gc, g_tot=g_tot,
                                  qk=qk * decay, lmat=jnp.where(strict, kk * decay * beta, 0.0)))
    for p in probs:
        p["n"] = -jnp.where(blk < 2, p["lmat"], 0.0)
    for level in range(1, int(math.log2(cs))):
        mask = (blk >= (1 << level)) & (blk < (2 << level))
        for p in probs:
            c_k = jnp.where(mask, p["lmat"], 0.0)
            y = c_k + _dot(c_k, p["n"])
            p["n"] = p["n"] - y - _dot(p["n"], y)
    for p in probs:
        j, hh, d = p["j"], p["hh"], p["d"]
        e_gc = jnp.exp(p["gc"])
        vb = p["v"] * p["beta"]
        kbg = p["k"] * (p["beta"] * e_gc)
        uw = _dot(p["n"], jnp.concatenate([vb, kbg], axis=1))
        wu = jnp.concatenate([kbg + uw[:, hd:], vb + uw[:, 0:hd]], axis=1)
        kd = p["k"] * jnp.exp(p["g_tot"] - p["gc"])
        top = _dot_tn(kd, wu)
        bot = _dot(p["qk"], wu)
        c = grp * group + j
        pq_s[hh, d, c, 0:hd, :] = top[:, 0:hd].astype(pq_s.dtype)
        pq_s[hh, d, c, hd:, :] = (p["q"] * e_gc - bot[:, 0:hd]).astype(pq_s.dtype)
        b_s[hh, d, c] = top[:, hd:]
        o_s[hh, d, pl.ds(r0 + j * cs, cs), :] = bot[:, hd:]
        gl_s[hh, d, c] = jnp.broadcast_to(jnp.exp(p["g_tot"]), (1, hd))


def _gdn_scan_step(c, hh, d, state, pq_s, b_s, gl_s, o_s):
    hd = GDN_HEAD_DIM
    r0 = pl.multiple_of(c * GDN_CHUNK, GDN_CHUNK)
    r = _dot(pq_s[hh, d, c], state)
    o_s[hh, d, pl.ds(r0, GDN_CHUNK), :] += r[hd:]
    return state * gl_s[hh, d, c] - r[0:hd] + b_s[hh, d, c]


def _gdn_kernel(q_ref, k_ref, v_ref, z_ref, wq_ref, wk_ref, wv_ref, gcol_ref, grow_ref, ng_ref, s0_ref,
                y_ref, sfin_ref, q_s, k_s, v_s, o_s, pq_s, b_s, gl_s, *, t_len, group, hp):
    hd = GDN_HEAD_DIM
    head0 = pl.program_id(1) * hp
    n_chunks = t_len // GDN_CHUNK
    q_s[...] = _l2norm_heads(_gdn_conv_silu(q_ref, wq_ref, t_len), hp) * (hd ** -0.5)
    k_s[...] = _l2norm_heads(_gdn_conv_silu(k_ref, wk_ref, t_len), hp)
    v_s[...] = _gdn_conv_silu(v_ref, wv_ref, t_len)

    def prepare(grp, carry):
        _gdn_prepare(grp, head0, q_s, k_s, v_s, gcol_ref, grow_ref, pq_s, b_s, gl_s, o_s, group=group, hp=hp)
        return carry

    lax.fori_loop(0, n_chunks // group, prepare, 0)

    def scan(i, states):
        new = []
        for hh in range(hp):
            new.append(_gdn_scan_step(i, hh, 0, states[2 * hh], pq_s, b_s, gl_s, o_s))
            new.append(_gdn_scan_step(n_chunks - 1 - i, hh, 1, states[2 * hh + 1], pq_s, b_s, gl_s, o_s))
        return tuple(new)

    init = tuple(s0_ref[d, hh] for hh in range(hp) for d in range(2))
    final = lax.fori_loop(0, n_chunks, scan, init)
    outs = []
    for hh in range(hp):
        sfin_ref[0, hh] = final[2 * hh]
        sfin_ref[1, hh] = final[2 * hh + 1]
        o = o_s[hh, 0] + o_s[hh, 1]
        outs.append(o * lax.rsqrt(jnp.mean(o * o, axis=-1, keepdims=True) + EPS) * ng_ref[...])
    o = outs[0] if hp == 1 else jnp.concatenate(outs, axis=1)
    y_ref[...] = (o * _silu(z_ref[...].astype(f32))).astype(y_ref.dtype)


def _gdn(u_a, gcols, grows, conv_w, norm_g, s0, t_len, n_seq, row_off, hp, group):
    hd = GDN_HEAD_DIM
    wd = hd * hp
    nb = GDN_HEADS // hp
    blk0 = row_off // t_len
    c_q = POOL_W // wd
    col = lambda base: pl.BlockSpec((t_len, wd), lambda s, h: (blk0 + s, base + h))
    cw = lambda base: pl.BlockSpec((3, wd), lambda s, h: (0, base + h))
    n_chunks = t_len // GDN_CHUNK
    state_spec = pl.BlockSpec((None, 2, hp, hd, hd), lambda s, h: (s, 0, h, 0, 0))
    return pl.pallas_call(
        functools.partial(_gdn_kernel, t_len=t_len, group=group, hp=hp),
        grid=(n_seq, nb),
        in_specs=[col(c_q), col(c_q + nb), col(c_q + 2 * nb), col(c_q + 3 * nb),
                  cw(0), cw(nb), cw(2 * nb),
                  pl.BlockSpec((t_len, B_W), lambda s, h: (blk0 + s, 0)),
                  pl.BlockSpec((n_chunks, GATE_LANES, GDN_CHUNK), lambda s, h: (blk0 + s, 0, 0)),
                  pl.BlockSpec((1, hd), lambda s, h: (0, 0)),
                  state_spec],
        out_specs=[pl.BlockSpec((t_len, wd), lambda s, h: (s, h)), state_spec],
        out_shape=[jax.ShapeDtypeStruct((n_seq * t_len, GDN_W), bf16),
                   jax.ShapeDtypeStruct((n_seq, 2, GDN_HEADS, hd, hd), f32)],
        scratch_shapes=[pltpu.VMEM((t_len, wd), f32)] * 3
        + [pltpu.VMEM((hp, 2, t_len, hd), f32),
           pltpu.VMEM((hp, 2, n_chunks, GDN_STEP_ROWS, hd), bf16),
           pltpu.VMEM((hp, 2, n_chunks, hd, hd), f32),
           pltpu.VMEM((hp, 2, n_chunks, 1, hd), f32)],
        compiler_params=_cparams(2),
        name=f"gdn_t{t_len}",
    )(u_a, u_a, u_a, u_a, conv_w, conv_w, conv_w, gcols, grows, norm_g.reshape(1, hd), s0)


def _ctx_attn_kernel(x_ref, o_ref):
    hd = NA_HEAD_DIM
    outs = []
    for h in range(NA_HEADS):
        q = x_ref[:, h * hd:(h + 1) * hd]
        k = x_ref[:, NA_W + h * hd:NA_W + (h + 1) * hd]
        v = x_ref[:, 2 * NA_W + h * hd:2 * NA_W + (h + 1) * hd]
        s = _dot_nt(q, k) * (hd ** -0.5)
        p = jnp.exp(s - jnp.max(s, axis=-1, keepdims=True))
        outs.append(_dot(p, v) / jnp.sum(p, axis=-1, keepdims=True))
    o_ref[...] = jnp.concatenate(outs, axis=1).astype(o_ref.dtype)


def _ctx_attention(u_c):
    return pl.pallas_call(
        _ctx_attn_kernel,
        grid=(BATCH,),
        in_specs=[pl.BlockSpec((SEQ, C_W), lambda b: (b, 0))],
        out_specs=pl.BlockSpec((SEQ, NA_W), lambda b: (b, 0)),
        out_shape=jax.ShapeDtypeStruct((N_CTX, NA_W), bf16),
        compiler_params=_cparams(1),
        name="ctx_attention",
    )(u_c)


HEADS_PER_STEP = 128 // NA_HEAD_DIM
NB_KEYS = NA_WR * GRID_W


def _na_kernel(q_ref, k_ref, v_ref, kc_ref, vc_ref, bias_ref, o_ref, k_s, v_s, kc_s, vc_s):
    hd = NA_HEAD_DIM
    scale = hd ** -0.5
    k_s[...] = k_ref[...].astype(bf16)
    v_s[...] = v_ref[...].astype(bf16)
    kc_s[...] = kc_ref[...].astype(bf16)
    vc_s[...] = vc_ref[...].astype(bf16)
    lane = lax.broadcasted_iota(jnp.int32, (1, HEADS_PER_STEP * hd), 1)

    def body(r, carry):
        start = jnp.clip(r - NA_WR // 2, 0, GRID_ROWS - NA_WR)
        case = start - r + NA_WR - 1
        q0 = pl.multiple_of(r * GRID_W, GRID_W)
        k0 = pl.multiple_of(start * GRID_W, GRID_W)
        q = q_ref[pl.ds(q0, GRID_W), :] * scale
        k_nb = k_s[pl.ds(k0, NB_KEYS), :]
        v_nb = v_s[pl.ds(k0, NB_KEYS), :]
        out = None
        for hh in range(HEADS_PER_STEP):
            own = (lane >= hh * hd) & (lane < (hh + 1) * hd)
            qh = jnp.where(own, q, 0.0).astype(bf16)
            s_nb = _dot_nt(qh, k_nb) + bias_ref[hh, case]
            s_cx = _dot_nt(qh, kc_s[...])
            mx = jnp.maximum(jnp.max(s_nb, axis=-1, keepdims=True), jnp.max(s_cx, axis=-1, keepdims=True))
            p_nb = jnp.exp(s_nb - mx)
            p_cx = jnp.exp(s_cx - mx)
            den = jnp.sum(p_nb, axis=-1, keepdims=True) + jnp.sum(p_cx, axis=-1, keepdims=True)
            o = (_dot(p_nb, v_nb) + _dot(p_cx, vc_s[...])) / den
            out = o if out is None else jnp.where(own, o, out)
        o_ref[pl.ds(q0, GRID_W), :] = out.astype(o_ref.dtype)
        return carry

    lax.fori_loop(0, GRID_ROWS, body, 0, unroll=2)


def _na_bias_tiles(rpb):
    col = np.arange(GRID_W)
    col_start = np.clip(col - NA_WC // 2, 0, GRID_W - NA_WC)
    col_mask = (col[None, :] >= col_start[:, None]) & (col[None, :] < col_start[:, None] + NA_WC)
    d_col = np.clip(col[None, :] - col[:, None], -(NA_WC - 1), NA_WC - 1) + NA_WC - 1
    onehot = (d_col[:, :, None] == np.arange(2 * NA_WC - 1)).astype(np.float32)
    picked = jnp.einsum("hrd,qkd->hrqk", rpb, onehot, precision=lax.Precision.HIGHEST)
    table = jnp.where(col_mask[None, None], picked, NEG_BIG)
    tiles = [table[:, case:case + NA_WR].transpose(0, 2, 1, 3).reshape(NA_HEADS, GRID_W, NB_KEYS)
             for case in range(NA_WR)]
    return jnp.stack(tiles, axis=1).astype(f32)


def _na_attention(u_c, k_ctx, v_ctx, bias_tiles):
    blk0 = N_CTX // DEC_SEQ
    nhp = NA_HEADS // HEADS_PER_STEP
    col = lambda base: pl.BlockSpec((DEC_SEQ, 128), lambda b, p: (blk0 + b, base + p))
    cache = pl.BlockSpec((None, PAST_LEN, 128), lambda b, p: (b, 0, p))
    return pl.pallas_call(
        _na_kernel,
        grid=(DEC_BATCH, nhp),
        in_specs=[col(0), col(nhp), col(2 * nhp), cache, cache,
                  pl.BlockSpec((HEADS_PER_STEP, NA_WR, GRID_W, NB_KEYS), lambda b, p: (p, 0, 0, 0))],
        out_specs=pl.BlockSpec((DEC_SEQ, 128), lambda b, p: (b, p)),
        out_shape=jax.ShapeDtypeStruct((N_LAT, NA_W), bf16),
        scratch_shapes=[pltpu.VMEM((DEC_SEQ, 128), bf16)] * 2 + [pltpu.VMEM((PAST_LEN, 128), bf16)] * 2,
        compiler_params=_cparams(2),
        name="na_attention",
    )(u_c, u_c, u_c, k_ctx, v_ctx, bias_tiles)


def _final_norm_kernel(x_ref, g_ref, o_ref):
    x = x_ref[...]
    o_ref[...] = x * lax.rsqrt(jnp.mean(x * x, axis=-1, keepdims=True) + EPS) * g_ref[...]


def _final_norm(x, g, n_rows, row_off):
    tm = 512
    blk0 = row_off // tm
    return pl.pallas_call(
        _final_norm_kernel,
        grid=(n_rows // tm,),
        in_specs=[pl.BlockSpec((tm, D_MODEL), lambda m: (blk0 + m, 0)),
                  pl.BlockSpec((1, D_MODEL), lambda m: (0, 0))],
        out_specs=pl.BlockSpec((tm, D_MODEL), lambda m: (m, 0)),
        out_shape=jax.ShapeDtypeStruct((n_rows, D_MODEL), f32),
        compiler_params=_cparams(1),
        name="final_norm",
    )(x, g.reshape(1, D_MODEL))


def kernel(x_prompt, x_sample, cache_na_k, cache_na_v, state_gdn, c, c_ctx, w_ada, b_ada, g_norm1, w_in, pool_w, pool_scale, gdn_conv, gdn_a_log, gdn_dt_bias, gdn_norm_g, na_rpb, w_branch_pool, w_branch_gdn, w_branch_na, w_out, g_norm2, w_up, ffn_conv, w_down, g_final):
    x = jnp.concatenate([x_prompt.reshape(N_CTX, D_MODEL), x_sample.reshape(N_LAT, D_MODEL)], axis=0)

    cvec = jnp.zeros((8, D_MODEL), f32).at[0].set(c_ctx).at[1:1 + DEC_BATCH].set(c)
    mods = _modulation(cvec, w_ada, b_ada)
    group_row = np.array([0] * (N_CTX // ROW_GROUP) + list(range(1, 1 + DEC_BATCH)))
    mods = mods[:, group_row].reshape(DEPTH, N_GROUPS, 6, 1, D_MODEL).transpose(0, 2, 1, 3, 4)

    zero_state = jnp.zeros((BATCH, 2, GDN_HEADS, GDN_HEAD_DIM, GDN_HEAD_DIM), f32)
    new_k, new_v, new_s = [], [], []
    for l in range(DEPTH):
        sh1, sc1, g1, sh2, sc2, g2 = (mods[l, j] for j in range(6))
        h = _norm_mod(x, g_norm1[l], sc1, sh1)
        u_a = _in_proj(h, w_in, l, 0, A_W, bf16, "in_proj_mix")
        u_b = _in_proj(h, w_in, l, A_W, GATE_LANES, f32, "in_proj_gates")
        u_c = _in_proj(h, w_in, l, A_W + GATE_LANES, C_W, f32, "in_proj_attn")
        u_g = _in_proj(h, w_in, l, A_W + GATE_LANES + C_W, G_W, bf16, "in_proj_branch_gates")

        y_pool = _pool(u_a, pool_w[l].astype(bf16), pool_scale[l])

        gcols, grows = _gate_prep(u_b, gdn_a_log[l], gdn_dt_bias[l])
        y_gdn_ctx, s_ctx = _gdn(u_a, gcols, grows, gdn_conv[l], gdn_norm_g[l], zero_state, SEQ, BATCH, 0,
                                hp=4, group=SEQ // GDN_CHUNK)
        y_gdn_lat, _ = _gdn(u_a, gcols, grows, gdn_conv[l], gdn_norm_g[l], state_gdn[:, l], DEC_SEQ, DEC_BATCH,
                            N_CTX, hp=2, group=8)

        y_na_ctx = _ctx_attention(u_c)
        y_na_lat = _na_attention(u_c, cache_na_k[:, l].reshape(DEC_BATCH, PAST_LEN, NA_W),
                                 cache_na_v[:, l].reshape(DEC_BATCH, PAST_LEN, NA_W), _na_bias_tiles(na_rpb[l]))

        merged = _merge(y_pool, y_gdn_ctx, y_gdn_lat, y_na_ctx, y_na_lat, w_branch_pool[l].astype(bf16),
                        w_branch_gdn[l].astype(bf16), w_branch_na[l].astype(bf16), u_g)
        x = _matmul_residual(merged, w_out[l].astype(bf16), x, g1, "out_proj")

        h = _norm_mod(x, g_norm2[l], sc2, sh2)
        act = _ffn_up_glu(h, w_up, l, ffn_conv[l])
        x = _matmul_residual(act, w_down[l].astype(bf16), x, g2, "ffn_down")

        new_k.append(u_c[:N_CTX, NA_W:2 * NA_W].reshape(BATCH, SEQ, NA_HEADS, NA_HEAD_DIM))
        new_v.append(u_c[:N_CTX, 2 * NA_W:].reshape(BATCH, SEQ, NA_HEADS, NA_HEAD_DIM))
        new_s.append(s_ctx)

    y_prompt = _final_norm(x, g_final, N_CTX, 0).reshape(BATCH, SEQ, D_MODEL)
    y_sample = _final_norm(x, g_final, N_LAT, N_CTX).reshape(DEC_BATCH, DEC_SEQ, D_MODEL)
    return (y_prompt, y_sample, jnp.stack(new_k, axis=1), jnp.stack(new_v, axis=1), jnp.stack(new_s, axis=1))
```

```python
import functools
import math

import numpy as np
import jax
import jax.numpy as jnp
from jax import lax
from jax.experimental import pallas as pl
from jax.experimental.pallas import tpu as pltpu

D_MODEL = 2048
BATCH = 16
SEQ = 256
DEPTH = 2
DEC_BATCH = 2
DEC_SEQ = 2048
PAST_LEN = 512
GRID_W = 64
POOL_GROUPS = 4
POOL_GC = 128
POOL_W = POOL_GROUPS * POOL_GC
POOL_WINDOWS = (2, 4, 8, 16)
GDN_HEADS = 8
GDN_HEAD_DIM = 128
GDN_W = GDN_HEADS * GDN_HEAD_DIM
GDN_CHUNK = 64
NA_HEADS = 8
NA_HEAD_DIM = 64
NA_W = NA_HEADS * NA_HEAD_DIM
NA_WR = 8
NA_WC = 16
D_FF = 5632
EPS = 1e-6

N_CTX = BATCH * SEQ
N_LAT = DEC_BATCH * DEC_SEQ
N_TOK = N_CTX + N_LAT
ROW_GROUP = DEC_SEQ
N_GROUPS = N_TOK // ROW_GROUP
SEQ_TILE = SEQ
HALO = 16
GRID_ROWS = DEC_SEQ // GRID_W
NEG_BIG = -1e30

A_W = POOL_W + 4 * GDN_W
B_W = 128
C_W = 3 * NA_W
G_W = 3 * D_MODEL

VMEM_LIMIT = 48 * 1024 * 1024

f32 = jnp.float32
bf16 = jnp.bfloat16


def _cparams(n_axes):
    return pltpu.CompilerParams(dimension_semantics=("arbitrary",) * n_axes, vmem_limit_bytes=VMEM_LIMIT)


def _dot(a, b):
    return jnp.dot(a.astype(bf16), b.astype(bf16), preferred_element_type=f32)


def _dot_nt(a, b):
    return lax.dot_general(a.astype(bf16), b.astype(bf16), (((1,), (1,)), ((), ())), preferred_element_type=f32)


def _dot_tn(a, b):
    return lax.dot_general(a.astype(bf16), b.astype(bf16), (((0,), (0,)), ((), ())), preferred_element_type=f32)


def _silu(x):
    return x * jax.nn.sigmoid(x)


def _col_tile(n, cap=1536):
    best = 128
    for t in range(128, cap + 1, 128):
        if n % t == 0:
            best = t
    return best


def _mod_kernel(c_ref, w_ref, b_ref, o_ref):
    a = _silu(c_ref[...])
    o_ref[...] = _dot(a, w_ref[...]) + b_ref[...]


def _modulation(cvec, w_ada, b_ada):
    tn = 1024
    n = 6 * D_MODEL
    return pl.pallas_call(
        _mod_kernel,
        grid=(DEPTH, n // tn),
        in_specs=[
            pl.BlockSpec((8, D_MODEL), lambda l, j: (0, 0)),
            pl.BlockSpec((None, D_MODEL, tn), lambda l, j: (l, 0, j)),
            pl.BlockSpec((None, 1, tn), lambda l, j: (l, 0, j)),
        ],
        out_specs=pl.BlockSpec((None, 8, tn), lambda l, j: (l, 0, j)),
        out_shape=jax.ShapeDtypeStruct((DEPTH, 8, n), f32),
        compiler_params=_cparams(2),
        name="modulation",
    )(cvec, w_ada, b_ada.reshape(DEPTH, 1, n))


def _mod_spec(tm, tn, col_of):
    return pl.BlockSpec((None, 1, tn), lambda m, n: ((m * tm) // ROW_GROUP, 0, col_of(n)))


def _norm_mod_kernel(x_ref, g_ref, sc_ref, sh_ref, o_ref):
    x = x_ref[...]
    y = x * lax.rsqrt(jnp.mean(x * x, axis=-1, keepdims=True) + EPS) * g_ref[...]
    o_ref[...] = (y * (1.0 + sc_ref[...]) + sh_ref[...]).astype(o_ref.dtype)


def _norm_mod(x, g, scale, shift):
    tm = 512
    mod = pl.BlockSpec((None, 1, D_MODEL), lambda m: ((m * tm) // ROW_GROUP, 0, 0))
    return pl.pallas_call(
        _norm_mod_kernel,
        grid=(N_TOK // tm,),
        in_specs=[
            pl.BlockSpec((tm, D_MODEL), lambda m: (m, 0)),
            pl.BlockSpec((1, D_MODEL), lambda m: (0, 0)),
            mod, mod,
        ],
        out_specs=pl.BlockSpec((tm, D_MODEL), lambda m: (m, 0)),
        out_shape=jax.ShapeDtypeStruct((N_TOK, D_MODEL), bf16),
        compiler_params=_cparams(1),
        name="norm_mod",
    )(x, g.reshape(1, D_MODEL), scale, shift)


LANE = 128


def _in_proj_kernel(x_ref, w_ref, wn_ref, o_ref, w_s, *, shift, n_valid):
    @pl.when(pl.program_id(1) == 0)
    def _():
        w = w_ref[...]
        if shift:
            w = jnp.concatenate([w[:, shift:], wn_ref[:, 0:shift]], axis=1)
        if n_valid < w.shape[1]:
            lane = lax.broadcasted_iota(jnp.int32, (1, w.shape[1]), 1)
            w = jnp.where(lane < n_valid, w, 0.0)
        w_s[...] = w.astype(bf16)

    o_ref[...] = jnp.dot(x_ref[...], w_s[...], preferred_element_type=f32).astype(o_ref.dtype)


def _in_proj(x, w_all, layer, col0, n, out_dtype, name):
    m, k = x.shape
    shift = col0 % LANE
    base = col0 - shift
    n_out = -(-n // LANE) * LANE
    tm = 1024
    tn = _col_tile(math.gcd(n_out, base) if base else n_out, 1024)
    j0 = base // tn
    per = tn // LANE
    return pl.pallas_call(
        functools.partial(_in_proj_kernel, shift=shift, n_valid=n if n < n_out else n_out),
        grid=(n_out // tn, m // tm),
        in_specs=[
            pl.BlockSpec((tm, k), lambda j, i: (i, 0)),
            pl.BlockSpec((None, k, tn), lambda j, i: (layer, 0, j0 + j)),
            pl.BlockSpec((None, k, LANE), lambda j, i: (layer, 0, (j0 + j + 1) * per)),
        ],
        out_specs=pl.BlockSpec((tm, tn), lambda j, i: (i, j)),
        out_shape=jax.ShapeDtypeStruct((m, n_out), out_dtype),
        scratch_shapes=[pltpu.VMEM((k, tn), bf16)],
        compiler_params=_cparams(2),
        name=name,
    )(x, w_all, w_all)


def _mm_res_kernel(a_ref, w_ref, x_ref, g_ref, o_ref):
    y = jnp.dot(a_ref[...], w_ref[...], preferred_element_type=f32)
    o_ref[...] = x_ref[...] + g_ref[...] * y


def _matmul_residual(a, w, x, gate, name):
    m, k = a.shape
    n = w.shape[1]
    tm = 1024
    tn = 1024 if k <= D_MODEL else 512
    return pl.pallas_call(
        _mm_res_kernel,
        grid=(m // tm, n // tn),
        in_specs=[
            pl.BlockSpec((tm, k), lambda i, j: (i, 0)),
            pl.BlockSpec((k, tn), lambda i, j: (0, j)),
            pl.BlockSpec((tm, tn), lambda i, j: (i, j)),
            _mod_spec(tm, tn, lambda j: j),
        ],
        out_specs=pl.BlockSpec((tm, tn), lambda i, j: (i, j)),
        out_shape=jax.ShapeDtypeStruct((m, n), f32),
        compiler_params=_cparams(2),
        name=name,
    )(a, w, x, gate)


MERGE_TM = 1024


def _merge_kernel(yp_ref, ygc_ref, ygl_ref, ync_ref, ynl_ref, wp_ref, wg_ref, wn_ref, g0_ref, g1_ref, g2_ref, o_ref):
    is_ctx = pl.program_id(0) < N_CTX // MERGE_TM

    def branch(y, w_ref, g_ref):
        return jax.nn.sigmoid(g_ref[...].astype(f32)) * jnp.dot(y, w_ref[...], preferred_element_type=f32)

    y_gdn = jnp.where(is_ctx, ygc_ref[...], ygl_ref[...])
    y_na = jnp.where(is_ctx, ync_ref[...], ynl_ref[...])
    acc = branch(yp_ref[...], wp_ref, g0_ref) + branch(y_gdn, wg_ref, g1_ref) + branch(y_na, wn_ref, g2_ref)
    o_ref[...] = acc.astype(o_ref.dtype)


def _merge(y_pool, y_gdn_ctx, y_gdn_lat, y_na_ctx, y_na_lat, w_pool, w_gdn, w_na, u_gate):
    tm = MERGE_TM
    tn = 1024
    nb = D_MODEL // tn
    n_ctx = N_CTX // tm
    row = lambda width: pl.BlockSpec((tm, width), lambda i, j: (i, 0))
    ctx = lambda width: pl.BlockSpec((tm, width), lambda i, j: (jnp.minimum(i, n_ctx - 1), 0))
    lat = lambda width: pl.BlockSpec((tm, width), lambda i, j: (jnp.maximum(i - n_ctx, 0), 0))
    wcol = lambda kdim: pl.BlockSpec((kdim, tn), lambda i, j: (0, j))
    gate = lambda b: pl.BlockSpec((tm, tn), lambda i, j: (i, j + b * nb))
    return pl.pallas_call(
        _merge_kernel,
        grid=(N_TOK // tm, nb),
        in_specs=[row(POOL_W), ctx(GDN_W), lat(GDN_W), ctx(NA_W), lat(NA_W),
                  wcol(POOL_W), wcol(GDN_W), wcol(NA_W), gate(0), gate(1), gate(2)],
        out_specs=pl.BlockSpec((tm, tn), lambda i, j: (i, j)),
        out_shape=jax.ShapeDtypeStruct((N_TOK, D_MODEL), bf16),
        compiler_params=_cparams(2),
        name="merge",
    )(y_pool, y_gdn_ctx, y_gdn_lat, y_na_ctx, y_na_lat, w_pool, w_gdn, w_na, u_gate, u_gate, u_gate)


def _tile_seq_len(m):
    return jnp.where(m * SEQ_TILE < N_CTX, SEQ, DEC_SEQ)


def _halo_specs(width, col_of):
    per = SEQ_TILE // HALO
    last = N_TOK // HALO - 1
    main = pl.BlockSpec((SEQ_TILE, width), lambda m, c: (m, col_of(c)))
    prev = pl.BlockSpec((HALO, width), lambda m, c: (jnp.maximum(m * per - 1, 0), col_of(c)))
    nxt = pl.BlockSpec((HALO, width), lambda m, c: (jnp.minimum((m + 1) * per, last), col_of(c)))
    return main, prev, nxt


def _tile_pos(m):
    t_len = _tile_seq_len(m)
    row = lax.broadcasted_iota(jnp.int32, (SEQ_TILE, 1), 0)
    return (m * SEQ_TILE + row) & (t_len - 1), t_len


def _pool_kernel(x_ref, xp_ref, xn_ref, w_ref, s_ref, o_ref, ext_ref):
    m = pl.program_id(0)
    pos, t_len = _tile_pos(m)
    first = ((m * SEQ_TILE) & (t_len - 1)) == 0
    last = (((m + 1) * SEQ_TILE) & (t_len - 1)) == 0
    x = x_ref[...].astype(f32)
    ext_ref[0:HALO, :] = jnp.where(first, 0.0, xp_ref[...].astype(f32))
    ext_ref[HALO:HALO + SEQ_TILE, :] = x
    ext_ref[HALO + SEQ_TILE:, :] = jnp.where(last, 0.0, xn_ref[...].astype(f32))
    outs = []
    for g, win in enumerate(POOL_WINDOWS):
        cols = slice(g * POOL_GC, (g + 1) * POOL_GC)
        acc = jnp.zeros((SEQ_TILE, POOL_GC), f32)
        for s in range(-(win // 2), win - win // 2):
            acc = acc + ext_ref[HALO + s:HALO + s + SEQ_TILE, cols]
        lo = jnp.maximum(pos - win // 2, 0)
        hi = jnp.minimum(pos + win - 1 - win // 2, t_len - 1)
        mean = acc / (hi - lo + 1).astype(f32)
        y = _dot(mean - x[:, cols], w_ref[g]) * s_ref[:, cols]
        outs.append(y)
    o_ref[...] = jnp.concatenate(outs, axis=1).astype(o_ref.dtype)


def _pool(u_a, pool_w, pool_scale):
    main, prev, nxt = _halo_specs(POOL_W, lambda c: 0)
    return pl.pallas_call(
        _pool_kernel,
        grid=(N_TOK // SEQ_TILE, 1),
        in_specs=[main, prev, nxt,
                  pl.BlockSpec((POOL_GROUPS, POOL_GC, POOL_GC), lambda m, c: (0, 0, 0)),
                  pl.BlockSpec((1, POOL_W), lambda m, c: (0, 0))],
        out_specs=pl.BlockSpec((SEQ_TILE, POOL_W), lambda m, c: (m, 0)),
        out_shape=jax.ShapeDtypeStruct((N_TOK, POOL_W), bf16),
        scratch_shapes=[pltpu.VMEM((SEQ_TILE + 2 * HALO, POOL_W), f32)],
        compiler_params=_cparams(2),
        name="pool",
    )(u_a, u_a, u_a, pool_w, pool_scale.reshape(1, POOL_W))


FFN_TM = 1024
FFN_TC = 512
FFN_SUB = 128


def _ffn_up_kernel(x_ref, xp_ref, xn_ref, wa_ref, wb_ref, ca_ref, cb_ref, o_ref, w_s):
    i = pl.program_id(1)
    n_sub = FFN_TC // FFN_SUB
    sub = FFN_SUB

    @pl.when(i == 0)
    def _():
        for s in range(n_sub):
            w_s[:, 2 * s * sub:(2 * s + 1) * sub] = wa_ref[:, s * sub:(s + 1) * sub].astype(bf16)
            w_s[:, (2 * s + 1) * sub:(2 * s + 2) * sub] = wb_ref[:, s * sub:(s + 1) * sub].astype(bf16)

    rows = FFN_TM + 2 * HALO
    x_ext = jnp.concatenate([xp_ref[...], x_ref[...], xn_ref[...]], axis=0)
    t_len = jnp.where(i * FFN_TM < N_CTX, SEQ, DEC_SEQ)
    pos = (i * FFN_TM + lax.broadcasted_iota(jnp.int32, (FFN_TM, 1), 0)) & (t_len - 1)
    first = pos == 0
    last = pos == t_len - 1
    for s in range(n_sub):
        ab = jnp.dot(x_ext, w_s[:, 2 * s * sub:(2 * s + 2) * sub], preferred_element_type=f32)
        taps = jnp.concatenate([ca_ref[:, s * sub:(s + 1) * sub], cb_ref[:, s * sub:(s + 1) * sub]], axis=1)
        before = jnp.where(first, 0.0, pltpu.roll(ab, 1, 0)[HALO:HALO + FFN_TM])
        after = jnp.where(last, 0.0, pltpu.roll(ab, rows - 1, 0)[HALO:HALO + FFN_TM])
        y = before * taps[0:1] + ab[HALO:HALO + FFN_TM] * taps[1:2] + after * taps[2:3]
        o_ref[:, s * sub:(s + 1) * sub] = (_silu(y[:, 0:sub]) * y[:, sub:]).astype(o_ref.dtype)


def _ffn_up_glu(h, w_up_all, layer, ffn_conv):
    nb = D_FF // FFN_TC
    per = FFN_TM // HALO
    last = N_TOK // HALO - 1
    return pl.pallas_call(
        _ffn_up_kernel,
        grid=(nb, N_TOK // FFN_TM),
        in_specs=[pl.BlockSpec((FFN_TM, D_MODEL), lambda j, i: (i, 0)),
                  pl.BlockSpec((HALO, D_MODEL), lambda j, i: (jnp.maximum(i * per - 1, 0), 0)),
                  pl.BlockSpec((HALO, D_MODEL), lambda j, i: (jnp.minimum((i + 1) * per, last), 0)),
                  pl.BlockSpec((None, D_MODEL, FFN_TC), lambda j, i: (layer, 0, j)),
                  pl.BlockSpec((None, D_MODEL, FFN_TC), lambda j, i: (layer, 0, j + nb)),
                  pl.BlockSpec((3, FFN_TC), lambda j, i: (0, j)),
                  pl.BlockSpec((3, FFN_TC), lambda j, i: (0, j + nb))],
        out_specs=pl.BlockSpec((FFN_TM, FFN_TC), lambda j, i: (i, j)),
        out_shape=jax.ShapeDtypeStruct((N_TOK, D_FF), bf16),
        scratch_shapes=[pltpu.VMEM((D_MODEL, 2 * FFN_TC), bf16)],
        compiler_params=_cparams(2),
        name="ffn_up_glu",
    )(h, h, h, w_up_all, w_up_all, ffn_conv, ffn_conv)


GATE_LANES = 4 * GDN_HEADS
CHUNKS_PER_TILE = SEQ_TILE // GDN_CHUNK


def _gate_prep_kernel(u_ref, alog_ref, dtb_ref, cols_ref, rows_ref):
    u = u_ref[...]
    lane = lax.broadcasted_iota(jnp.int32, (1, B_W), 1)
    x = u + dtb_ref[...]
    softplus = jnp.maximum(x, 0.0) + jnp.log1p(jnp.exp(-jnp.abs(x)))
    g = -jnp.exp(alog_ref[...]) * softplus
    r = lax.broadcasted_iota(jnp.int32, (SEQ_TILE, SEQ_TILE), 0)
    c = lax.broadcasted_iota(jnp.int32, (SEQ_TILE, SEQ_TILE), 1)
    shift = int(math.log2(GDN_CHUNK))
    same = (r >> shift) == (c >> shift)
    tri_f = jnp.where(same & (c <= r), 1.0, 0.0)
    tri_b = jnp.where(same & (c >= r), 1.0, 0.0)
    gc_f = jnp.dot(tri_f, g, preferred_element_type=f32, precision=lax.Precision.HIGHEST)
    gc_b = jnp.dot(tri_b, g, preferred_element_type=f32, precision=lax.Precision.HIGHEST)
    cols = jnp.where(lane < 2 * GDN_HEADS, jax.nn.sigmoid(u), jnp.where(lane < 3 * GDN_HEADS, gc_f, gc_b))
    cols_ref[...] = cols
    for j in range(CHUNKS_PER_TILE):
        t = cols[j * GDN_CHUNK:(j + 1) * GDN_CHUNK, :].T
        rows_ref[j] = t[0:GATE_LANES, :]


def _gate_prep(u_b, a_log, dt_bias):
    pad = lambda p: jnp.zeros((1, B_W), f32).at[0, 2 * GDN_HEADS:GATE_LANES].set(p.reshape(-1))
    n_tiles = N_TOK // SEQ_TILE
    return pl.pallas_call(
        _gate_prep_kernel,
        grid=(n_tiles,),
        in_specs=[pl.BlockSpec((SEQ_TILE, B_W), lambda m: (m, 0)),
                  pl.BlockSpec((1, B_W), lambda m: (0, 0)),
                  pl.BlockSpec((1, B_W), lambda m: (0, 0))],
        out_specs=[pl.BlockSpec((SEQ_TILE, B_W), lambda m: (m, 0)),
                   pl.BlockSpec((CHUNKS_PER_TILE, GATE_LANES, GDN_CHUNK), lambda m: (m, 0, 0))],
        out_shape=[jax.ShapeDtypeStruct((N_TOK, B_W), f32),
                   jax.ShapeDtypeStruct((N_TOK // GDN_CHUNK, GATE_LANES, GDN_CHUNK), f32)],
        compiler_params=_cparams(1),
        name="gdn_gate_prep",
    )(u_b, pad(a_log), pad(dt_bias))


def _gdn_conv_silu(x_ref, w_ref, t_len):
    x = x_ref[...].astype(f32)
    row = lax.broadcasted_iota(jnp.int32, (t_len, 1), 0)
    before = jnp.where(row == 0, 0.0, pltpu.roll(x, 1, 0))
    after = jnp.where(row == t_len - 1, 0.0, pltpu.roll(x, t_len - 1, 0))
    return _silu(before * w_ref[0:1, :] + x * w_ref[1:2, :] + after * w_ref[2:3, :])


def _l2norm_heads(x, n_heads):
    hd = GDN_HEAD_DIM
    parts = []
    for h in range(n_heads):
        xh = x[:, h * hd:(h + 1) * hd]
        parts.append(xh * lax.rsqrt(jnp.sum(xh * xh, axis=-1, keepdims=True) + EPS))
    return parts[0] if n_heads == 1 else jnp.concatenate(parts, axis=1)


GDN_STEP_ROWS = GDN_HEAD_DIM + GDN_CHUNK


def _gdn_prepare(grp, head0, q_s, k_s, v_s, gcol_ref, grow_ref, pq_s, b_s, gl_s, o_s, *, group, hp):
    cs, hd = GDN_CHUNK, GDN_HEAD_DIM
    rows = group * cs
    r0 = pl.multiple_of(grp * rows, rows)
    lane = lax.broadcasted_iota(jnp.int32, (1, B_W), 1)
    ri = lax.broadcasted_iota(jnp.int32, (cs, cs), 0)
    ci = lax.broadcasted_iota(jnp.int32, (cs, cs), 1)
    blk = ri ^ ci
    probs = []
    for j in range(group):
        rj = r0 + j * cs
        gates = gcol_ref[pl.ds(rj, cs), :]
        pick = lambda idx, gates=gates: jnp.sum(jnp.where(lane == idx, gates, 0.0), axis=1, keepdims=True)
        for hh in range(hp):
            cols = slice(hh * hd, (hh + 1) * hd)
            head = head0 + hh
            q = q_s[pl.ds(rj, cs), cols]
            k = k_s[pl.ds(rj, cs), cols]
            v = v_s[pl.ds(rj, cs), cols]
            kk = _dot_nt(k, k)
            qk = _dot_nt(q, k)
            for direction in range(2):
                beta = pick(direction * GDN_HEADS + head)
                dec_lane = (2 + direction) * GDN_HEADS + head
                gc = pick(dec_lane)
                gr = grow_ref[grp * group + j, pl.ds(dec_lane, 1), :]
                if direction == 0:
                    incl, strict = ri >= ci, ri > ci
                    g_tot = gc[cs - 1:cs, :]
                else:
                    incl, strict = ri <= ci, ri < ci
                    g_tot = gc[0:1, :]
                decay = jnp.exp(jnp.where(incl, gc - gr, -jnp.inf))
                probs.append(dict(j=j, hh=hh, d=direction, beta=beta, gc=gc, g_tot=g_tot,
                                  qk=qk * decay, lmat=jnp.where(strict, kk * decay * beta, 0.0)))
    for p in probs:
        p["n"] = -jnp.where(blk < 2, p["lmat"], 0.0)
    for level in range(1, int(math.log2(cs))):
        mask = (blk >= (1 << level)) & (blk < (2 << level))
        for p in probs:
            c_k = jnp.where(mask, p["lmat"], 0.0)
            y = c_k + _dot(c_k, p["n"])
            p["n"] = p["n"] - y - _dot(p["n"], y)
    for p in probs:
        j, hh, d = p["j"], p["hh"], p["d"]
        cols = slice(hh * hd, (hh + 1) * hd)
        q = q_s[pl.ds(r0 + j * cs, cs), cols]
        k = k_s[pl.ds(r0 + j * cs, cs), cols]
        v = v_s[pl.ds(r0 + j * cs, cs), cols]
        e_gc = jnp.exp(p["gc"])
        vb = v * p["beta"]
        kbg = k * (p["beta"] * e_gc)
        uw = _dot(p["n"], jnp.concatenate([vb, kbg], axis=1))
        wu = jnp.concatenate([kbg + uw[:, hd:], vb + uw[:, 0:hd]], axis=1)
        kd = k * jnp.exp(p["g_tot"] - p["gc"])
        top = _dot_tn(kd, wu)
        bot = _dot(p["qk"], wu)
        c = grp * group + j
        pq_s[hh, d, c, 0:hd, :] = top[:, 0:hd].astype(pq_s.dtype)
        pq_s[hh, d, c, hd:, :] = (q * e_gc - bot[:, 0:hd]).astype(pq_s.dtype)
        b_s[hh, d, c] = top[:, hd:]
        o_s[hh, d, pl.ds(r0 + j * cs, cs), :] = bot[:, hd:]
        gl_s[hh, d, c] = jnp.broadcast_to(jnp.exp(p["g_tot"]), (1, hd))


def _gdn_scan_step(c, hh, d, state, pq_s, b_s, gl_s, o_s):
    hd = GDN_HEAD_DIM
    r0 = pl.multiple_of(c * GDN_CHUNK, GDN_CHUNK)
    r = _dot(pq_s[hh, d, c], state)
    o_s[hh, d, pl.ds(r0, GDN_CHUNK), :] += r[hd:]
    return state * gl_s[hh, d, c] - r[0:hd] + b_s[hh, d, c]


def _gdn_kernel(q_ref, k_ref, v_ref, z_ref, wq_ref, wk_ref, wv_ref, gcol_ref, grow_ref, ng_ref, s0_ref,
                y_ref, sfin_ref, q_s, k_s, v_s, o_s, pq_s, b_s, gl_s, *, t_len, group, hp):
    hd = GDN_HEAD_DIM
    head0 = pl.program_id(1) * hp
    n_chunks = t_len // GDN_CHUNK
    q_s[...] = _l2norm_heads(_gdn_conv_silu(q_ref, wq_ref, t_len), hp) * (hd ** -0.5)
    k_s[...] = _l2norm_heads(_gdn_conv_silu(k_ref, wk_ref, t_len), hp)
    v_s[...] = _gdn_conv_silu(v_ref, wv_ref, t_len)

    def prepare(grp, carry):
        _gdn_prepare(grp, head0, q_s, k_s, v_s, gcol_ref, grow_ref, pq_s, b_s, gl_s, o_s, group=group, hp=hp)
        return carry

    lax.fori_loop(0, n_chunks // group, prepare, 0)

    def scan(i, states):
        new = []
        for hh in range(hp):
            new.append(_gdn_scan_step(i, hh, 0, states[2 * hh], pq_s, b_s, gl_s, o_s))
            new.append(_gdn_scan_step(n_chunks - 1 - i, hh, 1, states[2 * hh + 1], pq_s, b_s, gl_s, o_s))
        return tuple(new)

    init = tuple(s0_ref[d, hh] for hh in range(hp) for d in range(2))
    final = lax.fori_loop(0, n_chunks, scan, init)
    outs = []
    for hh in range(hp):
        sfin_ref[0, hh] = final[2 * hh]
        sfin_ref[1, hh] = final[2 * hh + 1]
        o = o_s[hh, 0] + o_s[hh, 1]
        outs.append(o * lax.rsqrt(jnp.mean(o * o, axis=-1, keepdims=True) + EPS) * ng_ref[...])
    o = outs[0] if hp == 1 else jnp.concatenate(outs, axis=1)
    y_ref[...] = (o * _silu(z_ref[...].astype(f32))).astype(y_ref.dtype)


def _gdn(u_a, gcols, grows, conv_w, norm_g, s0, t_len, n_seq, row_off, hp, group):
    hd = GDN_HEAD_DIM
    wd = hd * hp
    nb = GDN_HEADS // hp
    blk0 = row_off // t_len
    c_q = POOL_W // wd
    col = lambda base: pl.BlockSpec((t_len, wd), lambda s, h: (blk0 + s, base + h))
    cw = lambda base: pl.BlockSpec((3, wd), lambda s, h: (0, base + h))
    n_chunks = t_len // GDN_CHUNK
    state_spec = pl.BlockSpec((None, 2, hp, hd, hd), lambda s, h: (s, 0, h, 0, 0))
    return pl.pallas_call(
        functools.partial(_gdn_kernel, t_len=t_len, group=group, hp=hp),
        grid=(n_seq, nb),
        in_specs=[col(c_q), col(c_q + nb), col(c_q + 2 * nb), col(c_q + 3 * nb),
                  cw(0), cw(nb), cw(2 * nb),
                  pl.BlockSpec((t_len, B_W), lambda s, h: (blk0 + s, 0)),
                  pl.BlockSpec((n_chunks, GATE_LANES, GDN_CHUNK), lambda s, h: (blk0 + s, 0, 0)),
                  pl.BlockSpec((1, hd), lambda s, h: (0, 0)),
                  state_spec],
        out_specs=[pl.BlockSpec((t_len, wd), lambda s, h: (s, h)), state_spec],
        out_shape=[jax.ShapeDtypeStruct((n_seq * t_len, GDN_W), bf16),
                   jax.ShapeDtypeStruct((n_seq, 2, GDN_HEADS, hd, hd), f32)],
        scratch_shapes=[pltpu.VMEM((t_len, wd), f32)] * 3
        + [pltpu.VMEM((hp, 2, t_len, hd), f32),
           pltpu.VMEM((hp, 2, n_chunks, GDN_STEP_ROWS, hd), bf16),
           pltpu.VMEM((hp, 2, n_chunks, hd, hd), f32),
           pltpu.VMEM((hp, 2, n_chunks, 1, hd), f32)],
        compiler_params=_cparams(2),
        name=f"gdn_t{t_len}",
    )(u_a, u_a, u_a, u_a, conv_w, conv_w, conv_w, gcols, grows, norm_g.reshape(1, hd), s0)


def _ctx_attn_kernel(x_ref, o_ref):
    hd = NA_HEAD_DIM
    outs = []
    for h in range(NA_HEADS):
        q = x_ref[:, h * hd:(h + 1) * hd]
        k = x_ref[:, NA_W + h * hd:NA_W + (h + 1) * hd]
        v = x_ref[:, 2 * NA_W + h * hd:2 * NA_W + (h + 1) * hd]
        s = _dot_nt(q, k) * (hd ** -0.5)
        p = jnp.exp(s - jnp.max(s, axis=-1, keepdims=True))
        outs.append(_dot(p, v) / jnp.sum(p, axis=-1, keepdims=True))
    o_ref[...] = jnp.concatenate(outs, axis=1).astype(o_ref.dtype)


def _ctx_attention(u_c):
    return pl.pallas_call(
        _ctx_attn_kernel,
        grid=(BATCH,),
        in_specs=[pl.BlockSpec((SEQ, C_W), lambda b: (b, 0))],
        out_specs=pl.BlockSpec((SEQ, NA_W), lambda b: (b, 0)),
        out_shape=jax.ShapeDtypeStruct((N_CTX, NA_W), bf16),
        compiler_params=_cparams(1),
        name="ctx_attention",
    )(u_c)


HEADS_PER_STEP = 128 // NA_HEAD_DIM
NB_KEYS = NA_WR * GRID_W


def _na_kernel(q_ref, k_ref, v_ref, kc_ref, vc_ref, bias_ref, o_ref, k_s, v_s, kc_s, vc_s):
    hd = NA_HEAD_DIM
    scale = hd ** -0.5
    k_s[...] = k_ref[...].astype(bf16)
    v_s[...] = v_ref[...].astype(bf16)
    kc_s[...] = kc_ref[...].astype(bf16)
    vc_s[...] = vc_ref[...].astype(bf16)
    lane = lax.broadcasted_iota(jnp.int32, (1, HEADS_PER_STEP * hd), 1)

    def body(r, carry):
        start = jnp.clip(r - NA_WR // 2, 0, GRID_ROWS - NA_WR)
        case = start - r + NA_WR - 1
        q0 = pl.multiple_of(r * GRID_W, GRID_W)
        k0 = pl.multiple_of(start * GRID_W, GRID_W)
        q = q_ref[pl.ds(q0, GRID_W), :] * scale
        k_nb = k_s[pl.ds(k0, NB_KEYS), :]
        v_nb = v_s[pl.ds(k0, NB_KEYS), :]
        own = [(lane >= hh * hd) & (lane < (hh + 1) * hd) for hh in range(HEADS_PER_STEP)]
        qh = jnp.concatenate([jnp.where(m, q, 0.0) for m in own], axis=0).astype(bf16)
        bias = bias_ref[:, case].reshape(HEADS_PER_STEP * GRID_W, NB_KEYS)
        s_nb = _dot_nt(qh, k_nb) + bias
        s_cx = _dot_nt(qh, kc_s[...])
        mx = jnp.maximum(jnp.max(s_nb, axis=-1, keepdims=True), jnp.max(s_cx, axis=-1, keepdims=True))
        p_nb = jnp.exp(s_nb - mx)
        p_cx = jnp.exp(s_cx - mx)
        den = jnp.sum(p_nb, axis=-1, keepdims=True) + jnp.sum(p_cx, axis=-1, keepdims=True)
        o = (_dot(p_nb, v_nb) + _dot(p_cx, vc_s[...])) / den
        out = o[0:GRID_W]
        for hh in range(1, HEADS_PER_STEP):
            out = jnp.where(own[hh], o[hh * GRID_W:(hh + 1) * GRID_W], out)
        o_ref[pl.ds(q0, GRID_W), :] = out.astype(o_ref.dtype)
        return carry

    lax.fori_loop(0, GRID_ROWS, body, 0, unroll=2)


def _na_bias_tiles(rpb):
    col = np.arange(GRID_W)
    col_start = np.clip(col - NA_WC // 2, 0, GRID_W - NA_WC)
    col_mask = (col[None, :] >= col_start[:, None]) & (col[None, :] < col_start[:, None] + NA_WC)
    d_col = np.clip(col[None, :] - col[:, None], -(NA_WC - 1), NA_WC - 1) + NA_WC - 1
    onehot = (d_col[:, :, None] == np.arange(2 * NA_WC - 1)).astype(np.float32)
    picked = jnp.einsum("hrd,qkd->hrqk", rpb, onehot, precision=lax.Precision.HIGHEST)
    table = jnp.where(col_mask[None, None], picked, NEG_BIG)
    tiles = [table[:, case:case + NA_WR].transpose(0, 2, 1, 3).reshape(NA_HEADS, GRID_W, NB_KEYS)
             for case in range(NA_WR)]
    return jnp.stack(tiles, axis=1).astype(f32)


def _na_attention(u_c, k_ctx, v_ctx, bias_tiles):
    blk0 = N_CTX // DEC_SEQ
    nhp = NA_HEADS // HEADS_PER_STEP
    col = lambda base: pl.BlockSpec((DEC_SEQ, 128), lambda b, p: (blk0 + b, base + p))
    cache = pl.BlockSpec((None, PAST_LEN, 128), lambda b, p: (b, 0, p))
    return pl.pallas_call(
        _na_kernel,
        grid=(DEC_BATCH, nhp),
        in_specs=[col(0), col(nhp), col(2 * nhp), cache, cache,
                  pl.BlockSpec((HEADS_PER_STEP, NA_WR, GRID_W, NB_KEYS), lambda b, p: (p, 0, 0, 0))],
        out_specs=pl.BlockSpec((DEC_SEQ, 128), lambda b, p: (b, p)),
        out_shape=jax.ShapeDtypeStruct((N_LAT, NA_W), bf16),
        scratch_shapes=[pltpu.VMEM((DEC_SEQ, 128), bf16)] * 2 + [pltpu.VMEM((PAST_LEN, 128), bf16)] * 2,
        compiler_params=_cparams(2),
        name="na_attention",
    )(u_c, u_c, u_c, k_ctx, v_ctx, bias_tiles)


def _final_norm_kernel(x_ref, g_ref, o_ref):
    x = x_ref[...]
    o_ref[...] = x * lax.rsqrt(jnp.mean(x * x, axis=-1, keepdims=True) + EPS) * g_ref[...]


def _final_norm(x, g, n_rows, row_off):
    tm = 512
    blk0 = row_off // tm
    return pl.pallas_call(
        _final_norm_kernel,
        grid=(n_rows // tm,),
        in_specs=[pl.BlockSpec((tm, D_MODEL), lambda m: (blk0 + m, 0)),
                  pl.BlockSpec((1, D_MODEL), lambda m: (0, 0))],
        out_specs=pl.BlockSpec((tm, D_MODEL), lambda m: (m, 0)),
        out_shape=jax.ShapeDtypeStruct((n_rows, D_MODEL), f32),
        compiler_params=_cparams(1),
        name="final_norm",
    )(x, g.reshape(1, D_MODEL))


def kernel(x_prompt, x_sample, cache_na_k, cache_na_v, state_gdn, c, c_ctx, w_ada, b_ada, g_norm1, w_in, pool_w, pool_scale, gdn_conv, gdn_a_log, gdn_dt_bias, gdn_norm_g, na_rpb, w_branch_pool, w_branch_gdn, w_branch_na, w_out, g_norm2, w_up, ffn_conv, w_down, g_final):
    x = jnp.concatenate([x_prompt.reshape(N_CTX, D_MODEL), x_sample.reshape(N_LAT, D_MODEL)], axis=0)

    cvec = jnp.zeros((8, D_MODEL), f32).at[0].set(c_ctx).at[1:1 + DEC_BATCH].set(c)
    mods = _modulation(cvec, w_ada, b_ada)
    group_row = np.array([0] * (N_CTX // ROW_GROUP) + list(range(1, 1 + DEC_BATCH)))
    mods = mods[:, group_row].reshape(DEPTH, N_GROUPS, 6, 1, D_MODEL).transpose(0, 2, 1, 3, 4)

    w_in_bf = w_in.astype(bf16)
    zero_state = jnp.zeros((BATCH, 2, GDN_HEADS, GDN_HEAD_DIM, GDN_HEAD_DIM), f32)
    new_k, new_v, new_s = [], [], []
    for l in range(DEPTH):
        sh1, sc1, g1, sh2, sc2, g2 = (mods[l, j] for j in range(6))
        h = _norm_mod(x, g_norm1[l], sc1, sh1)
        u_a = _in_proj(h, w_in_bf, l, 0, A_W, bf16, "in_proj_mix")
        u_b = _in_proj(h, w_in_bf, l, A_W, GATE_LANES, f32, "in_proj_gates")
        u_c = _in_proj(h, w_in_bf, l, A_W + GATE_LANES, C_W, f32, "in_proj_attn")
        u_g = _in_proj(h, w_in_bf, l, A_W + GATE_LANES + C_W, G_W, bf16, "in_proj_branch_gates")

        y_pool = _pool(u_a, pool_w[l].astype(bf16), pool_scale[l])

        gcols, grows = _gate_prep(u_b, gdn_a_log[l], gdn_dt_bias[l])
        y_gdn_ctx, s_ctx = _gdn(u_a, gcols, grows, gdn_conv[l], gdn_norm_g[l], zero_state, SEQ, BATCH, 0,
                                hp=4, group=SEQ // GDN_CHUNK)
        y_gdn_lat, _ = _gdn(u_a, gcols, grows, gdn_conv[l], gdn_norm_g[l], state_gdn[:, l], DEC_SEQ, DEC_BATCH,
                            N_CTX, hp=2, group=8)

        y_na_ctx = _ctx_attention(u_c)
        y_na_lat = _na_attention(u_c, cache_na_k[:, l].reshape(DEC_BATCH, PAST_LEN, NA_W),
                                 cache_na_v[:, l].reshape(DEC_BATCH, PAST_LEN, NA_W), _na_bias_tiles(na_rpb[l]))

        merged = _merge(y_pool, y_gdn_ctx, y_gdn_lat, y_na_ctx, y_na_lat, w_branch_pool[l].astype(bf16),
                        w_branch_gdn[l].astype(bf16), w_branch_na[l].astype(bf16), u_g)
        x = _matmul_residual(merged, w_out[l].astype(bf16), x, g1, "out_proj")

        h = _norm_mod(x, g_norm2[l], sc2, sh2)
        act = _ffn_up_glu(h, w_up, l, ffn_conv[l])
        x = _matmul_residual(act, w_down[l].astype(bf16), x, g2, "ffn_down")

        new_k.append(u_c[:N_CTX, NA_W:2 * NA_W].reshape(BATCH, SEQ, NA_HEADS, NA_HEAD_DIM))
        new_v.append(u_c[:N_CTX, 2 * NA_W:].reshape(BATCH, SEQ, NA_HEADS, NA_HEAD_DIM))
        new_s.append(s_ctx)

    y_prompt = _final_norm(x, g_final, N_CTX, 0).reshape(BATCH, SEQ, D_MODEL)
    y_sample = _final_norm(x, g_final, N_LAT, N_CTX).reshape(DEC_BATCH, DEC_SEQ, D_MODEL)
    return (y_prompt, y_sample, jnp.stack(new_k, axis=1), jnp.stack(new_v, axis=1), jnp.stack(new_s, axis=1))
```

```python
import functools
import math

import numpy as np
import jax
import jax.numpy as jnp
from jax import lax
from jax.experimental import pallas as pl
from jax.experimental.pallas import tpu as pltpu

D_MODEL = 2048
BATCH = 16
SEQ = 256
DEPTH = 2
DEC_BATCH = 2
DEC_SEQ = 2048
PAST_LEN = 512
GRID_W = 64
POOL_GROUPS = 4
POOL_GC = 128
POOL_W = POOL_GROUPS * POOL_GC
POOL_WINDOWS = (2, 4, 8, 16)
GDN_HEADS = 8
GDN_HEAD_DIM = 128
GDN_W = GDN_HEADS * GDN_HEAD_DIM
GDN_CHUNK = 64
NA_HEADS = 8
NA_HEAD_DIM = 64
NA_W = NA_HEADS * NA_HEAD_DIM
NA_WR = 8
NA_WC = 16
D_FF = 5632
EPS = 1e-6

N_CTX = BATCH * SEQ
N_LAT = DEC_BATCH * DEC_SEQ
N_TOK = N_CTX + N_LAT
ROW_GROUP = DEC_SEQ
N_GROUPS = N_TOK // ROW_GROUP
SEQ_TILE = SEQ
HALO = 16
GRID_ROWS = DEC_SEQ // GRID_W
NEG_BIG = -1e30

A_W = POOL_W + 4 * GDN_W
B_W = 128
C_W = 3 * NA_W
G_W = 3 * D_MODEL

VMEM_LIMIT = 48 * 1024 * 1024

f32 = jnp.float32
bf16 = jnp.bfloat16


def _cparams(n_axes):
    return pltpu.CompilerParams(dimension_semantics=("arbitrary",) * n_axes, vmem_limit_bytes=VMEM_LIMIT)


def _dot(a, b):
    return jnp.dot(a.astype(bf16), b.astype(bf16), preferred_element_type=f32)


def _dot_nt(a, b):
    return lax.dot_general(a.astype(bf16), b.astype(bf16), (((1,), (1,)), ((), ())), preferred_element_type=f32)


def _dot_tn(a, b):
    return lax.dot_general(a.astype(bf16), b.astype(bf16), (((0,), (0,)), ((), ())), preferred_element_type=f32)


def _silu(x):
    return x * jax.nn.sigmoid(x)


def _col_tile(n, cap=1536):
    best = 128
    for t in range(128, cap + 1, 128):
        if n % t == 0:
            best = t
    return best


def _mod_kernel(c_ref, w_ref, b_ref, o_ref):
    a = _silu(c_ref[...])
    o_ref[...] = _dot(a, w_ref[...]) + b_ref[...]


def _modulation(cvec, w_ada, b_ada):
    tn = 1024
    n = 6 * D_MODEL
    return pl.pallas_call(
        _mod_kernel,
        grid=(DEPTH, n // tn),
        in_specs=[
            pl.BlockSpec((8, D_MODEL), lambda l, j: (0, 0)),
            pl.BlockSpec((None, D_MODEL, tn), lambda l, j: (l, 0, j)),
            pl.BlockSpec((None, 1, tn), lambda l, j: (l, 0, j)),
        ],
        out_specs=pl.BlockSpec((None, 8, tn), lambda l, j: (l, 0, j)),
        out_shape=jax.ShapeDtypeStruct((DEPTH, 8, n), f32),
        compiler_params=_cparams(2),
        name="modulation",
    )(cvec, w_ada, b_ada.reshape(DEPTH, 1, n))


def _mod_spec(tm, tn, col_of):
    return pl.BlockSpec((None, 1, tn), lambda m, n: ((m * tm) // ROW_GROUP, 0, col_of(n)))


def _token_rows(x, tm, width, col_of, n_grid_axes):
    if n_grid_axes == 1:
        wrap = lambda f: (lambda m: f(m, 0))
    else:
        wrap = lambda f: f
    if not isinstance(x, tuple):
        return [pl.BlockSpec((tm, width), wrap(lambda m, c: (m, col_of(c))))], [x]
    n_ctx = N_CTX // tm
    ctx_map = lambda m, c: (jnp.minimum(m, n_ctx - 1), jnp.where(m < n_ctx, col_of(c), 0))
    lat_map = lambda m, c: (jnp.maximum(m - n_ctx, 0), jnp.where(m >= n_ctx, col_of(c), 0))
    return [pl.BlockSpec((tm, width), wrap(ctx_map)), pl.BlockSpec((tm, width), wrap(lat_map))], list(x)


def _read_rows(refs, tm):
    if len(refs) == 1:
        return refs[0][...]
    return jnp.where(pl.program_id(0) < N_CTX // tm, refs[0][...], refs[1][...])


NORM_TM = 512


def _norm_mod_kernel(*refs):
    g_ref, sc_ref, sh_ref, o_ref = refs[-4:]
    x = _read_rows(refs[:-4], NORM_TM)
    y = x * lax.rsqrt(jnp.mean(x * x, axis=-1, keepdims=True) + EPS) * g_ref[...]
    o_ref[...] = (y * (1.0 + sc_ref[...]) + sh_ref[...]).astype(o_ref.dtype)


def _norm_mod(x, g, scale, shift):
    tm = NORM_TM
    mod = pl.BlockSpec((None, 1, D_MODEL), lambda m: ((m * tm) // ROW_GROUP, 0, 0))
    x_specs, x_ops = _token_rows(x, tm, D_MODEL, lambda c: 0, 1)
    return pl.pallas_call(
        _norm_mod_kernel,
        grid=(N_TOK // tm,),
        in_specs=[*x_specs, pl.BlockSpec((1, D_MODEL), lambda m: (0, 0)), mod, mod],
        out_specs=pl.BlockSpec((tm, D_MODEL), lambda m: (m, 0)),
        out_shape=jax.ShapeDtypeStruct((N_TOK, D_MODEL), bf16),
        compiler_params=_cparams(1),
        name="norm_mod",
    )(*x_ops, g.reshape(1, D_MODEL), scale, shift)


LANE = 128


def _in_proj_kernel(x_ref, w_ref, wn_ref, o_ref, w_s, *, shift, n_valid):
    @pl.when(pl.program_id(1) == 0)
    def _():
        w = w_ref[...]
        if shift:
            w = jnp.concatenate([w[:, shift:], wn_ref[:, 0:shift]], axis=1)
        if n_valid < w.shape[1]:
            lane = lax.broadcasted_iota(jnp.int32, (1, w.shape[1]), 1)
            w = jnp.where(lane < n_valid, w, 0.0)
        w_s[...] = w.astype(bf16)

    o_ref[...] = jnp.dot(x_ref[...], w_s[...], preferred_element_type=f32).astype(o_ref.dtype)


def _in_proj(x, w_all, layer, col0, n, out_dtype, name):
    m, k = x.shape
    shift = col0 % LANE
    base = col0 - shift
    n_out = -(-n // LANE) * LANE
    tm = 1024
    tn = _col_tile(math.gcd(n_out, base) if base else n_out, 1024)
    j0 = base // tn
    per = tn // LANE
    return pl.pallas_call(
        functools.partial(_in_proj_kernel, shift=shift, n_valid=n if n < n_out else n_out),
        grid=(n_out // tn, m // tm),
        in_specs=[
            pl.BlockSpec((tm, k), lambda j, i: (i, 0)),
            pl.BlockSpec((None, k, tn), lambda j, i: (layer, 0, j0 + j)),
            pl.BlockSpec((None, k, LANE), lambda j, i: (layer, 0, (j0 + j + 1) * per)),
        ],
        out_specs=pl.BlockSpec((tm, tn), lambda j, i: (i, j)),
        out_shape=jax.ShapeDtypeStruct((m, n_out), out_dtype),
        scratch_shapes=[pltpu.VMEM((k, tn), bf16)],
        compiler_params=_cparams(2),
        name=name,
    )(x, w_all, w_all)


RES_TM = 1024


def _mm_res_kernel(a_ref, w_ref, g_ref, *refs):
    o_ref = refs[-1]
    y = jnp.dot(a_ref[...], w_ref[...], preferred_element_type=f32)
    o_ref[...] = _read_rows(refs[:-1], RES_TM) + g_ref[...] * y


def _matmul_residual(a, w, x, gate, name):
    m, k = a.shape
    n = w.shape[1]
    tm = RES_TM
    tn = 1024 if k <= D_MODEL else 512
    x_specs, x_ops = _token_rows(x, tm, tn, lambda j: j, 2)
    return pl.pallas_call(
        _mm_res_kernel,
        grid=(m // tm, n // tn),
        in_specs=[
            pl.BlockSpec((tm, k), lambda i, j: (i, 0)),
            pl.BlockSpec((k, tn), lambda i, j: (0, j)),
            _mod_spec(tm, tn, lambda j: j),
            *x_specs,
        ],
        out_specs=pl.BlockSpec((tm, tn), lambda i, j: (i, j)),
        out_shape=jax.ShapeDtypeStruct((m, n), f32),
        compiler_params=_cparams(2),
        name=name,
    )(a, w, gate, *x_ops)


MERGE_TM = 1024


def _merge_kernel(yp_ref, ygc_ref, ygl_ref, ync_ref, ynl_ref, wp_ref, wg_ref, wn_ref, g0_ref, g1_ref, g2_ref, o_ref):
    is_ctx = pl.program_id(0) < N_CTX // MERGE_TM

    def branch(y, w_ref, g_ref):
        return jax.nn.sigmoid(g_ref[...].astype(f32)) * jnp.dot(y, w_ref[...], preferred_element_type=f32)

    y_gdn = jnp.where(is_ctx, ygc_ref[...], ygl_ref[...])
    y_na = jnp.where(is_ctx, ync_ref[...], ynl_ref[...])
    acc = branch(yp_ref[...], wp_ref, g0_ref) + branch(y_gdn, wg_ref, g1_ref) + branch(y_na, wn_ref, g2_ref)
    o_ref[...] = acc.astype(o_ref.dtype)


def _merge(y_pool, y_gdn_ctx, y_gdn_lat, y_na_ctx, y_na_lat, w_pool, w_gdn, w_na, u_gate):
    tm = MERGE_TM
    tn = 1024
    nb = D_MODEL // tn
    n_ctx = N_CTX // tm
    row = lambda width: pl.BlockSpec((tm, width), lambda i, j: (i, 0))
    ctx = lambda width: pl.BlockSpec((tm, width), lambda i, j: (jnp.minimum(i, n_ctx - 1), 0))
    lat = lambda width: pl.BlockSpec((tm, width), lambda i, j: (jnp.maximum(i - n_ctx, 0), 0))
    wcol = lambda kdim: pl.BlockSpec((kdim, tn), lambda i, j: (0, j))
    gate = lambda b: pl.BlockSpec((tm, tn), lambda i, j: (i, j + b * nb))
    return pl.pallas_call(
        _merge_kernel,
        grid=(N_TOK // tm, nb),
        in_specs=[row(POOL_W), ctx(GDN_W), lat(GDN_W), ctx(NA_W), lat(NA_W),
                  wcol(POOL_W), wcol(GDN_W), wcol(NA_W), gate(0), gate(1), gate(2)],
        out_specs=pl.BlockSpec((tm, tn), lambda i, j: (i, j)),
        out_shape=jax.ShapeDtypeStruct((N_TOK, D_MODEL), bf16),
        compiler_params=_cparams(2),
        name="merge",
    )(y_pool, y_gdn_ctx, y_gdn_lat, y_na_ctx, y_na_lat, w_pool, w_gdn, w_na, u_gate, u_gate, u_gate)


def _tile_seq_len(m):
    return jnp.where(m * SEQ_TILE < N_CTX, SEQ, DEC_SEQ)


def _halo_specs(width, col_of):
    per = SEQ_TILE // HALO
    last = N_TOK // HALO - 1
    main = pl.BlockSpec((SEQ_TILE, width), lambda m, c: (m, col_of(c)))
    prev = pl.BlockSpec((HALO, width), lambda m, c: (jnp.maximum(m * per - 1, 0), col_of(c)))
    nxt = pl.BlockSpec((HALO, width), lambda m, c: (jnp.minimum((m + 1) * per, last), col_of(c)))
    return main, prev, nxt


def _tile_pos(m):
    t_len = _tile_seq_len(m)
    row = lax.broadcasted_iota(jnp.int32, (SEQ_TILE, 1), 0)
    return (m * SEQ_TILE + row) & (t_len - 1), t_len


def _pool_kernel(x_ref, xp_ref, xn_ref, w_ref, s_ref, o_ref, ext_ref):
    m = pl.program_id(0)
    pos, t_len = _tile_pos(m)
    first = ((m * SEQ_TILE) & (t_len - 1)) == 0
    last = (((m + 1) * SEQ_TILE) & (t_len - 1)) == 0
    x = x_ref[...].astype(f32)
    ext_ref[0:HALO, :] = jnp.where(first, 0.0, xp_ref[...].astype(f32))
    ext_ref[HALO:HALO + SEQ_TILE, :] = x
    ext_ref[HALO + SEQ_TILE:, :] = jnp.where(last, 0.0, xn_ref[...].astype(f32))
    outs = []
    for g, win in enumerate(POOL_WINDOWS):
        cols = slice(g * POOL_GC, (g + 1) * POOL_GC)
        acc = jnp.zeros((SEQ_TILE, POOL_GC), f32)
        for s in range(-(win // 2), win - win // 2):
            acc = acc + ext_ref[HALO + s:HALO + s + SEQ_TILE, cols]
        lo = jnp.maximum(pos - win // 2, 0)
        hi = jnp.minimum(pos + win - 1 - win // 2, t_len - 1)
        mean = acc / (hi - lo + 1).astype(f32)
        y = _dot(mean - x[:, cols], w_ref[g]) * s_ref[:, cols]
        outs.append(y)
    o_ref[...] = jnp.concatenate(outs, axis=1).astype(o_ref.dtype)


def _pool(u_a, pool_w, pool_scale):
    main, prev, nxt = _halo_specs(POOL_W, lambda c: 0)
    return pl.pallas_call(
        _pool_kernel,
        grid=(N_TOK // SEQ_TILE, 1),
        in_specs=[main, prev, nxt,
                  pl.BlockSpec((POOL_GROUPS, POOL_GC, POOL_GC), lambda m, c: (0, 0, 0)),
                  pl.BlockSpec((1, POOL_W), lambda m, c: (0, 0))],
        out_specs=pl.BlockSpec((SEQ_TILE, POOL_W), lambda m, c: (m, 0)),
        out_shape=jax.ShapeDtypeStruct((N_TOK, POOL_W), bf16),
        scratch_shapes=[pltpu.VMEM((SEQ_TILE + 2 * HALO, POOL_W), f32)],
        compiler_params=_cparams(2),
        name="pool",
    )(u_a, u_a, u_a, pool_w, pool_scale.reshape(1, POOL_W))


FFN_TM = 1024
FFN_TC = 512
FFN_SUB = 128


def _ffn_up_kernel(x_ref, xp_ref, xn_ref, wa_ref, wb_ref, ca_ref, cb_ref, o_ref, w_s):
    i = pl.program_id(1)
    n_sub = FFN_TC // FFN_SUB
    sub = FFN_SUB

    @pl.when(i == 0)
    def _():
        for s in range(n_sub):
            w_s[:, 2 * s * sub:(2 * s + 1) * sub] = wa_ref[:, s * sub:(s + 1) * sub].astype(bf16)
            w_s[:, (2 * s + 1) * sub:(2 * s + 2) * sub] = wb_ref[:, s * sub:(s + 1) * sub].astype(bf16)

    rows = FFN_TM + 2 * HALO
    x_ext = jnp.concatenate([xp_ref[...], x_ref[...], xn_ref[...]], axis=0)
    t_len = jnp.where(i * FFN_TM < N_CTX, SEQ, DEC_SEQ)
    pos = (i * FFN_TM + lax.broadcasted_iota(jnp.int32, (FFN_TM, 1), 0)) & (t_len - 1)
    first = pos == 0
    last = pos == t_len - 1
    for s in range(n_sub):
        ab = jnp.dot(x_ext, w_s[:, 2 * s * sub:(2 * s + 2) * sub], preferred_element_type=f32)
        taps = jnp.concatenate([ca_ref[:, s * sub:(s + 1) * sub], cb_ref[:, s * sub:(s + 1) * sub]], axis=1)
        before = jnp.where(first, 0.0, pltpu.roll(ab, 1, 0)[HALO:HALO + FFN_TM])
        after = jnp.where(last, 0.0, pltpu.roll(ab, rows - 1, 0)[HALO:HALO + FFN_TM])
        y = before * taps[0:1] + ab[HALO:HALO + FFN_TM] * taps[1:2] + after * taps[2:3]
        o_ref[:, s * sub:(s + 1) * sub] = (_silu(y[:, 0:sub]) * y[:, sub:]).astype(o_ref.dtype)


def _ffn_up_glu(h, w_up_all, layer, ffn_conv):
    nb = D_FF // FFN_TC
    per = FFN_TM // HALO
    last = N_TOK // HALO - 1
    return pl.pallas_call(
        _ffn_up_kernel,
        grid=(nb, N_TOK // FFN_TM),
        in_specs=[pl.BlockSpec((FFN_TM, D_MODEL), lambda j, i: (i, 0)),
                  pl.BlockSpec((HALO, D_MODEL), lambda j, i: (jnp.maximum(i * per - 1, 0), 0)),
                  pl.BlockSpec((HALO, D_MODEL), lambda j, i: (jnp.minimum((i + 1) * per, last), 0)),
                  pl.BlockSpec((None, D_MODEL, FFN_TC), lambda j, i: (layer, 0, j)),
                  pl.BlockSpec((None, D_MODEL, FFN_TC), lambda j, i: (layer, 0, j + nb)),
                  pl.BlockSpec((3, FFN_TC), lambda j, i: (0, j)),
                  pl.BlockSpec((3, FFN_TC), lambda j, i: (0, j + nb))],
        out_specs=pl.BlockSpec((FFN_TM, FFN_TC), lambda j, i: (i, j)),
        out_shape=jax.ShapeDtypeStruct((N_TOK, D_FF), bf16),
        scratch_shapes=[pltpu.VMEM((D_MODEL, 2 * FFN_TC), bf16)],
        compiler_params=_cparams(2),
        name="ffn_up_glu",
    )(h, h, h, w_up_all, w_up_all, ffn_conv, ffn_conv)


GATE_LANES = 4 * GDN_HEADS
CHUNKS_PER_TILE = SEQ_TILE // GDN_CHUNK


def _gate_prep_kernel(u_ref, alog_ref, dtb_ref, cols_ref, rows_ref):
    u = u_ref[...]
    lane = lax.broadcasted_iota(jnp.int32, (1, B_W), 1)
    x = u + dtb_ref[...]
    softplus = jnp.maximum(x, 0.0) + jnp.log1p(jnp.exp(-jnp.abs(x)))
    g = -jnp.exp(alog_ref[...]) * softplus
    r = lax.broadcasted_iota(jnp.int32, (SEQ_TILE, SEQ_TILE), 0)
    c = lax.broadcasted_iota(jnp.int32, (SEQ_TILE, SEQ_TILE), 1)
    shift = int(math.log2(GDN_CHUNK))
    same = (r >> shift) == (c >> shift)
    tri_f = jnp.where(same & (c <= r), 1.0, 0.0)
    tri_b = jnp.where(same & (c >= r), 1.0, 0.0)
    gc_f = jnp.dot(tri_f, g, preferred_element_type=f32, precision=lax.Precision.HIGHEST)
    gc_b = jnp.dot(tri_b, g, preferred_element_type=f32, precision=lax.Precision.HIGHEST)
    cols = jnp.where(lane < 2 * GDN_HEADS, jax.nn.sigmoid(u), jnp.where(lane < 3 * GDN_HEADS, gc_f, gc_b))
    cols_ref[...] = cols
    for j in range(CHUNKS_PER_TILE):
        t = cols[j * GDN_CHUNK:(j + 1) * GDN_CHUNK, :].T
        rows_ref[j] = t[0:GATE_LANES, :]


def _gate_prep(u_b, a_log, dt_bias):
    pad = lambda p: jnp.zeros((1, B_W), f32).at[0, 2 * GDN_HEADS:GATE_LANES].set(p.reshape(-1))
    n_tiles = N_TOK // SEQ_TILE
    return pl.pallas_call(
        _gate_prep_kernel,
        grid=(n_tiles,),
        in_specs=[pl.BlockSpec((SEQ_TILE, B_W), lambda m: (m, 0)),
                  pl.BlockSpec((1, B_W), lambda m: (0, 0)),
                  pl.BlockSpec((1, B_W), lambda m: (0, 0))],
        out_specs=[pl.BlockSpec((SEQ_TILE, B_W), lambda m: (m, 0)),
                   pl.BlockSpec((CHUNKS_PER_TILE, GATE_LANES, GDN_CHUNK), lambda m: (m, 0, 0))],
        out_shape=[jax.ShapeDtypeStruct((N_TOK, B_W), f32),
                   jax.ShapeDtypeStruct((N_TOK // GDN_CHUNK, GATE_LANES, GDN_CHUNK), f32)],
        compiler_params=_cparams(1),
        name="gdn_gate_prep",
    )(u_b, pad(a_log), pad(dt_bias))


def _gdn_conv_silu(x_ref, w_ref, t_len):
    x = x_ref[...].astype(f32)
    row = lax.broadcasted_iota(jnp.int32, (t_len, 1), 0)
    before = jnp.where(row == 0, 0.0, pltpu.roll(x, 1, 0))
    after = jnp.where(row == t_len - 1, 0.0, pltpu.roll(x, t_len - 1, 0))
    return _silu(before * w_ref[0:1, :] + x * w_ref[1:2, :] + after * w_ref[2:3, :])


def _l2norm_heads(x, n_heads):
    hd = GDN_HEAD_DIM
    parts = []
    for h in range(n_heads):
        xh = x[:, h * hd:(h + 1) * hd]
        parts.append(xh * lax.rsqrt(jnp.sum(xh * xh, axis=-1, keepdims=True) + EPS))
    return parts[0] if n_heads == 1 else jnp.concatenate(parts, axis=1)


GDN_STEP_ROWS = GDN_HEAD_DIM + GDN_CHUNK


def _gdn_prepare(grp, head0, q_s, k_s, v_s, gcol_ref, grow_ref, pq_s, b_s, gl_s, o_s, *, group, hp):
    cs, hd = GDN_CHUNK, GDN_HEAD_DIM
    rows = group * cs
    r0 = pl.multiple_of(grp * rows, rows)
    lane = lax.broadcasted_iota(jnp.int32, (1, B_W), 1)
    ri = lax.broadcasted_iota(jnp.int32, (cs, cs), 0)
    ci = lax.broadcasted_iota(jnp.int32, (cs, cs), 1)
    blk = ri ^ ci
    probs = []
    for j in range(group):
        rj = r0 + j * cs
        gates = gcol_ref[pl.ds(rj, cs), :]
        pick = lambda idx, gates=gates: jnp.sum(jnp.where(lane == idx, gates, 0.0), axis=1, keepdims=True)
        for hh in range(hp):
            cols = slice(hh * hd, (hh + 1) * hd)
            head = head0 + hh
            q = q_s[pl.ds(rj, cs), cols]
            k = k_s[pl.ds(rj, cs), cols]
            v = v_s[pl.ds(rj, cs), cols]
            kk = _dot_nt(k, k)
            qk = _dot_nt(q, k)
            for direction in range(2):
                beta = pick(direction * GDN_HEADS + head)
                dec_lane = (2 + direction) * GDN_HEADS + head
                gc = pick(dec_lane)
                gr = grow_ref[grp * group + j, pl.ds(dec_lane, 1), :]
                if direction == 0:
                    incl, strict = ri >= ci, ri > ci
                    g_tot = gc[cs - 1:cs, :]
                else:
                    incl, strict = ri <= ci, ri < ci
                    g_tot = gc[0:1, :]
                decay = jnp.exp(jnp.where(incl, gc - gr, -jnp.inf))
                probs.append(dict(j=j, hh=hh, d=direction, beta=beta, gc=gc, g_tot=g_tot,
                                  qk=qk * decay, lmat=jnp.where(strict, kk * decay * beta, 0.0)))
    for p in probs:
        p["n"] = -jnp.where(blk < 2, p["lmat"], 0.0)
    for level in range(1, int(math.log2(cs))):
        mask = (blk >= (1 << level)) & (blk < (2 << level))
        for p in probs:
            c_k = jnp.where(mask, p["lmat"], 0.0)
            y = c_k + _dot(c_k, p["n"])
            p["n"] = p["n"] - y - _dot(p["n"], y)
    for p in probs:
        j, hh, d = p["j"], p["hh"], p["d"]
        cols = slice(hh * hd, (hh + 1) * hd)
        q = q_s[pl.ds(r0 + j * cs, cs), cols]
        k = k_s[pl.ds(r0 + j * cs, cs), cols]
        v = v_s[pl.ds(r0 + j * cs, cs), cols]
        e_gc = jnp.exp(p["gc"])
        vb = v * p["beta"]
        kbg = k * (p["beta"] * e_gc)
        uw = _dot(p["n"], jnp.concatenate([vb, kbg], axis=1))
        wu = jnp.concatenate([kbg + uw[:, hd:], vb + uw[:, 0:hd]], axis=1)
        kd = k * jnp.exp(p["g_tot"] - p["gc"])
        top = _dot_tn(kd, wu)
        bot = _dot(p["qk"], wu)
        c = grp * group + j
        pq_s[hh, d, c, 0:hd, :] = top[:, 0:hd].astype(pq_s.dtype)
        pq_s[hh, d, c, hd:, :] = (q * e_gc - bot[:, 0:hd]).astype(pq_s.dtype)
        b_s[hh, d, c] = top[:, hd:]
        o_s[hh, d, pl.ds(r0 + j * cs, cs), :] = bot[:, hd:]
        gl_s[hh, d, c] = jnp.broadcast_to(jnp.exp(p["g_tot"]), (1, hd))


def _gdn_scan_step(c, hh, d, state, pq_s, b_s, gl_s, o_s):
    hd = GDN_HEAD_DIM
    r0 = pl.multiple_of(c * GDN_CHUNK, GDN_CHUNK)
    r = _dot(pq_s[hh, d, c], state)
    o_s[hh, d, pl.ds(r0, GDN_CHUNK), :] += r[hd:]
    return state * gl_s[hh, d, c] - r[0:hd] + b_s[hh, d, c]


def _gdn_kernel(q_ref, k_ref, v_ref, z_ref, wq_ref, wk_ref, wv_ref, gcol_ref, grow_ref, ng_ref, s0_ref,
                y_ref, sfin_ref, q_s, k_s, v_s, o_s, pq_s, b_s, gl_s, *, t_len, group, hp):
    hd = GDN_HEAD_DIM
    head0 = pl.program_id(1) * hp
    n_chunks = t_len // GDN_CHUNK
    q_s[...] = _l2norm_heads(_gdn_conv_silu(q_ref, wq_ref, t_len), hp) * (hd ** -0.5)
    k_s[...] = _l2norm_heads(_gdn_conv_silu(k_ref, wk_ref, t_len), hp)
    v_s[...] = _gdn_conv_silu(v_ref, wv_ref, t_len)

    def prepare(grp, carry):
        _gdn_prepare(grp, head0, q_s, k_s, v_s, gcol_ref, grow_ref, pq_s, b_s, gl_s, o_s, group=group, hp=hp)
        return carry

    lax.fori_loop(0, n_chunks // group, prepare, 0)

    def scan(i, states):
        new = []
        for hh in range(hp):
            new.append(_gdn_scan_step(i, hh, 0, states[2 * hh], pq_s, b_s, gl_s, o_s))
            new.append(_gdn_scan_step(n_chunks - 1 - i, hh, 1, states[2 * hh + 1], pq_s, b_s, gl_s, o_s))
        return tuple(new)

    init = tuple(s0_ref[d, hh] for hh in range(hp) for d in range(2))
    final = lax.fori_loop(0, n_chunks, scan, init)
    outs = []
    for hh in range(hp):
        sfin_ref[0, hh] = final[2 * hh]
        sfin_ref[1, hh] = final[2 * hh + 1]
        o = o_s[hh, 0] + o_s[hh, 1]
        outs.append(o * lax.rsqrt(jnp.mean(o * o, axis=-1, keepdims=True) + EPS) * ng_ref[...])
    o = outs[0] if hp == 1 else jnp.concatenate(outs, axis=1)
    y_ref[...] = (o * _silu(z_ref[...].astype(f32))).astype(y_ref.dtype)


def _gdn(u_a, gcols, grows, conv_w, norm_g, s0, t_len, n_seq, row_off, hp, group):
    hd = GDN_HEAD_DIM
    wd = hd * hp
    nb = GDN_HEADS // hp
    blk0 = row_off // t_len
    c_q = POOL_W // wd
    col = lambda base: pl.BlockSpec((t_len, wd), lambda s, h: (blk0 + s, base + h))
    cw = lambda base: pl.BlockSpec((3, wd), lambda s, h: (0, base + h))
    n_chunks = t_len // GDN_CHUNK
    state_spec = pl.BlockSpec((None, 2, hp, hd, hd), lambda s, h: (s, 0, h, 0, 0))
    return pl.pallas_call(
        functools.partial(_gdn_kernel, t_len=t_len, group=group, hp=hp),
        grid=(n_seq, nb),
        in_specs=[col(c_q), col(c_q + nb), col(c_q + 2 * nb), col(c_q + 3 * nb),
                  cw(0), cw(nb), cw(2 * nb),
                  pl.BlockSpec((t_len, B_W), lambda s, h: (blk0 + s, 0)),
                  pl.BlockSpec((n_chunks, GATE_LANES, GDN_CHUNK), lambda s, h: (blk0 + s, 0, 0)),
                  pl.BlockSpec((1, hd), lambda s, h: (0, 0)),
                  state_spec],
        out_specs=[pl.BlockSpec((t_len, wd), lambda s, h: (s, h)), state_spec],
        out_shape=[jax.ShapeDtypeStruct((n_seq * t_len, GDN_W), bf16),
                   jax.ShapeDtypeStruct((n_seq, 2, GDN_HEADS, hd, hd), f32)],
        scratch_shapes=[pltpu.VMEM((t_len, wd), f32)] * 3
        + [pltpu.VMEM((hp, 2, t_len, hd), f32),
           pltpu.VMEM((hp, 2, n_chunks, GDN_STEP_ROWS, hd), bf16),
           pltpu.VMEM((hp, 2, n_chunks, hd, hd), f32),
           pltpu.VMEM((hp, 2, n_chunks, 1, hd), f32)],
        compiler_params=_cparams(2),
        name=f"gdn_t{t_len}",
    )(u_a, u_a, u_a, u_a, conv_w, conv_w, conv_w, gcols, grows, norm_g.reshape(1, hd), s0)


def _ctx_attn_kernel(x_ref, o_ref):
    hd = NA_HEAD_DIM
    outs = []
    for h in range(NA_HEADS):
        q = x_ref[:, h * hd:(h + 1) * hd]
        k = x_ref[:, NA_W + h * hd:NA_W + (h + 1) * hd]
        v = x_ref[:, 2 * NA_W + h * hd:2 * NA_W + (h + 1) * hd]
        s = _dot_nt(q, k) * (hd ** -0.5)
        p = jnp.exp(s - jnp.max(s, axis=-1, keepdims=True))
        outs.append(_dot(p, v) / jnp.sum(p, axis=-1, keepdims=True))
    o_ref[...] = jnp.concatenate(outs, axis=1).astype(o_ref.dtype)


def _ctx_attention(u_c):
    return pl.pallas_call(
        _ctx_attn_kernel,
        grid=(BATCH,),
        in_specs=[pl.BlockSpec((SEQ, C_W), lambda b: (b, 0))],
        out_specs=pl.BlockSpec((SEQ, NA_W), lambda b: (b, 0)),
        out_shape=jax.ShapeDtypeStruct((N_CTX, NA_W), bf16),
        compiler_params=_cparams(1),
        name="ctx_attention",
    )(u_c)


HEADS_PER_STEP = 128 // NA_HEAD_DIM
NB_KEYS = NA_WR * GRID_W


def _na_kernel(q_ref, k_ref, v_ref, kc_ref, vc_ref, bias_ref, o_ref, k_s, v_s, kc_s, vc_s):
    hd = NA_HEAD_DIM
    scale = hd ** -0.5
    k_s[...] = k_ref[...].astype(bf16)
    v_s[...] = v_ref[...].astype(bf16)
    kc_s[...] = kc_ref[...].astype(bf16)
    vc_s[...] = vc_ref[...].astype(bf16)
    lane = lax.broadcasted_iota(jnp.int32, (1, HEADS_PER_STEP * hd), 1)

    def body(r, carry):
        start = jnp.clip(r - NA_WR // 2, 0, GRID_ROWS - NA_WR)
        case = start - r + NA_WR - 1
        q0 = pl.multiple_of(r * GRID_W, GRID_W)
        k0 = pl.multiple_of(start * GRID_W, GRID_W)
        q = q_ref[pl.ds(q0, GRID_W), :] * scale
        k_nb = k_s[pl.ds(k0, NB_KEYS), :]
        v_nb = v_s[pl.ds(k0, NB_KEYS), :]
        own = [(lane >= hh * hd) & (lane < (hh + 1) * hd) for hh in range(HEADS_PER_STEP)]
        qh = jnp.concatenate([jnp.where(m, q, 0.0) for m in own], axis=0).astype(bf16)
        bias = bias_ref[:, case].reshape(HEADS_PER_STEP * GRID_W, NB_KEYS)
        s_nb = _dot_nt(qh, k_nb) + bias
        s_cx = _dot_nt(qh, kc_s[...])
        mx = jnp.maximum(jnp.max(s_nb, axis=-1, keepdims=True), jnp.max(s_cx, axis=-1, keepdims=True))
        p_nb = jnp.exp(s_nb - mx)
        p_cx = jnp.exp(s_cx - mx)
        den = jnp.sum(p_nb, axis=-1, keepdims=True) + jnp.sum(p_cx, axis=-1, keepdims=True)
        o = (_dot(p_nb, v_nb) + _dot(p_cx, vc_s[...])) / den
        out = o[0:GRID_W]
        for hh in range(1, HEADS_PER_STEP):
            out = jnp.where(own[hh], o[hh * GRID_W:(hh + 1) * GRID_W], out)
        o_ref[pl.ds(q0, GRID_W), :] = out.astype(o_ref.dtype)
        return carry

    lax.fori_loop(0, GRID_ROWS, body, 0, unroll=2)


def _na_bias_tiles(rpb):
    col = np.arange(GRID_W)
    col_start = np.clip(col - NA_WC // 2, 0, GRID_W - NA_WC)
    col_mask = (col[None, :] >= col_start[:, None]) & (col[None, :] < col_start[:, None] + NA_WC)
    d_col = np.clip(col[None, :] - col[:, None], -(NA_WC - 1), NA_WC - 1) + NA_WC - 1
    onehot = (d_col[:, :, None] == np.arange(2 * NA_WC - 1)).astype(np.float32)
    picked = jnp.einsum("hrd,qkd->hrqk", rpb, onehot, precision=lax.Precision.HIGHEST)
    table = jnp.where(col_mask[None, None], picked, NEG_BIG)
    tiles = [table[:, case:case + NA_WR].transpose(0, 2, 1, 3).reshape(NA_HEADS, GRID_W, NB_KEYS)
             for case in range(NA_WR)]
    return jnp.stack(tiles, axis=1).astype(f32)


def _na_attention(u_c, k_ctx, v_ctx, bias_tiles):
    blk0 = N_CTX // DEC_SEQ
    nhp = NA_HEADS // HEADS_PER_STEP
    col = lambda base: pl.BlockSpec((DEC_SEQ, 128), lambda b, p: (blk0 + b, base + p))
    cache = pl.BlockSpec((None, PAST_LEN, 128), lambda b, p: (b, 0, p))
    return pl.pallas_call(
        _na_kernel,
        grid=(DEC_BATCH, nhp),
        in_specs=[col(0), col(nhp), col(2 * nhp), cache, cache,
                  pl.BlockSpec((HEADS_PER_STEP, NA_WR, GRID_W, NB_KEYS), lambda b, p: (p, 0, 0, 0))],
        out_specs=pl.BlockSpec((DEC_SEQ, 128), lambda b, p: (b, p)),
        out_shape=jax.ShapeDtypeStruct((N_LAT, NA_W), bf16),
        scratch_shapes=[pltpu.VMEM((DEC_SEQ, 128), bf16)] * 2 + [pltpu.VMEM((PAST_LEN, 128), bf16)] * 2,
        compiler_params=_cparams(2),
        name="na_attention",
    )(u_c, u_c, u_c, k_ctx, v_ctx, bias_tiles)


def _final_norm_kernel(x_ref, g_ref, o_ref):
    x = x_ref[...]
    o_ref[...] = x * lax.rsqrt(jnp.mean(x * x, axis=-1, keepdims=True) + EPS) * g_ref[...]


def _final_norm(x, g, n_rows, row_off):
    tm = 512
    blk0 = row_off // tm
    return pl.pallas_call(
        _final_norm_kernel,
        grid=(n_rows // tm,),
        in_specs=[pl.BlockSpec((tm, D_MODEL), lambda m: (blk0 + m, 0)),
                  pl.BlockSpec((1, D_MODEL), lambda m: (0, 0))],
        out_specs=pl.BlockSpec((tm, D_MODEL), lambda m: (m, 0)),
        out_shape=jax.ShapeDtypeStruct((n_rows, D_MODEL), f32),
        compiler_params=_cparams(1),
        name="final_norm",
    )(x, g.reshape(1, D_MODEL))


def kernel(x_prompt, x_sample, cache_na_k, cache_na_v, state_gdn, c, c_ctx, w_ada, b_ada, g_norm1, w_in, pool_w, pool_scale, gdn_conv, gdn_a_log, gdn_dt_bias, gdn_norm_g, na_rpb, w_branch_pool, w_branch_gdn, w_branch_na, w_out, g_norm2, w_up, ffn_conv, w_down, g_final):
    x = (x_prompt.reshape(N_CTX, D_MODEL), x_sample.reshape(N_LAT, D_MODEL))

    cvec = jnp.zeros((8, D_MODEL), f32).at[0].set(c_ctx).at[1:1 + DEC_BATCH].set(c)
    mods = _modulation(cvec, w_ada, b_ada)
    group_row = np.array([0] * (N_CTX // ROW_GROUP) + list(range(1, 1 + DEC_BATCH)))
    mods = mods[:, group_row].reshape(DEPTH, N_GROUPS, 6, 1, D_MODEL).transpose(0, 2, 1, 3, 4)

    w_in_bf = jnp.pad(w_in, ((0, 0), (0, 0), (0, -w_in.shape[-1] % LANE))).astype(bf16)
    zero_state = jnp.zeros((BATCH, 2, GDN_HEADS, GDN_HEAD_DIM, GDN_HEAD_DIM), f32)
    new_k, new_v, new_s = [], [], []
    for l in range(DEPTH):
        sh1, sc1, g1, sh2, sc2, g2 = (mods[l, j] for j in range(6))
        h = _norm_mod(x, g_norm1[l], sc1, sh1)
        u_a = _in_proj(h, w_in_bf, l, 0, A_W, bf16, "in_proj_mix")
        u_b = _in_proj(h, w_in_bf, l, A_W, GATE_LANES, f32, "in_proj_gates")
        u_c = _in_proj(h, w_in_bf, l, A_W + GATE_LANES, C_W, f32, "in_proj_attn")
        u_g = _in_proj(h, w_in_bf, l, A_W + GATE_LANES + C_W, G_W, bf16, "in_proj_branch_gates")

        y_pool = _pool(u_a, pool_w[l].astype(bf16), pool_scale[l])

        gcols, grows = _gate_prep(u_b, gdn_a_log[l], gdn_dt_bias[l])
        y_gdn_ctx, s_ctx = _gdn(u_a, gcols, grows, gdn_conv[l], gdn_norm_g[l], zero_state, SEQ, BATCH, 0,
                                hp=4, group=SEQ // GDN_CHUNK)
        y_gdn_lat, _ = _gdn(u_a, gcols, grows, gdn_conv[l], gdn_norm_g[l], state_gdn[:, l], DEC_SEQ, DEC_BATCH,
                            N_CTX, hp=2, group=8)

        y_na_ctx = _ctx_attention(u_c)
        y_na_lat = _na_attention(u_c, cache_na_k[:, l].reshape(DEC_BATCH, PAST_LEN, NA_W),
                                 cache_na_v[:, l].reshape(DEC_BATCH, PAST_LEN, NA_W), _na_bias_tiles(na_rpb[l]))

        merged = _merge(y_pool, y_gdn_ctx, y_gdn_lat, y_na_ctx, y_na_lat, w_branch_pool[l].astype(bf16),
                        w_branch_gdn[l].astype(bf16), w_branch_na[l].astype(bf16), u_g)
        x = _matmul_residual(merged, w_out[l].astype(bf16), x, g1, "out_proj")

        h = _norm_mod(x, g_norm2[l], sc2, sh2)
        act = _ffn_up_glu(h, w_up, l, ffn_conv[l])
        x = _matmul_residual(act, w_down[l].astype(bf16), x, g2, "ffn_down")

        new_k.append(u_c[:N_CTX, NA_W:2 * NA_W].reshape(BATCH, SEQ, NA_HEADS, NA_HEAD_DIM))
        new_v.append(u_c[:N_CTX, 2 * NA_W:].reshape(BATCH, SEQ, NA_HEADS, NA_HEAD_DIM))
        new_s.append(s_ctx)

    y_prompt = _final_norm(x, g_final, N_CTX, 0).reshape(BATCH, SEQ, D_MODEL)
    y_sample = _final_norm(x, g_final, N_LAT, N_CTX).reshape(DEC_BATCH, DEC_SEQ, D_MODEL)
    return (y_prompt, y_sample, jnp.stack(new_k, axis=1), jnp.stack(new_v, axis=1), jnp.stack(new_s, axis=1))
```

```python
import functools
import math

import numpy as np
import jax
import jax.numpy as jnp
from jax import lax
from jax.experimental import pallas as pl
from jax.experimental.pallas import tpu as pltpu

D_MODEL = 2048
BATCH = 16
SEQ = 256
DEPTH = 2
DEC_BATCH = 2
DEC_SEQ = 2048
PAST_LEN = 512
GRID_W = 64
POOL_GROUPS = 4
POOL_GC = 128
POOL_W = POOL_GROUPS * POOL_GC
POOL_WINDOWS = (2, 4, 8, 16)
GDN_HEADS = 8
GDN_HEAD_DIM = 128
GDN_W = GDN_HEADS * GDN_HEAD_DIM
GDN_CHUNK = 64
NA_HEADS = 8
NA_HEAD_DIM = 64
NA_W = NA_HEADS * NA_HEAD_DIM
NA_WR = 8
NA_WC = 16
D_FF = 5632
EPS = 1e-6

N_CTX = BATCH * SEQ
N_LAT = DEC_BATCH * DEC_SEQ
N_TOK = N_CTX + N_LAT
ROW_GROUP = DEC_SEQ
N_GROUPS = N_TOK // ROW_GROUP
SEQ_TILE = SEQ
HALO = 16
GRID_ROWS = DEC_SEQ // GRID_W
NEG_BIG = -1e30

A_W = POOL_W + 4 * GDN_W
B_W = 128
C_W = 3 * NA_W
G_W = 3 * D_MODEL

VMEM_LIMIT = 48 * 1024 * 1024

f32 = jnp.float32
bf16 = jnp.bfloat16


def _cparams(n_axes):
    return pltpu.CompilerParams(dimension_semantics=("arbitrary",) * n_axes, vmem_limit_bytes=VMEM_LIMIT)


def _dot(a, b):
    return jnp.dot(a.astype(bf16), b.astype(bf16), preferred_element_type=f32)


def _dot_nt(a, b):
    return lax.dot_general(a.astype(bf16), b.astype(bf16), (((1,), (1,)), ((), ())), preferred_element_type=f32)


def _dot_tn(a, b):
    return lax.dot_general(a.astype(bf16), b.astype(bf16), (((0,), (0,)), ((), ())), preferred_element_type=f32)


def _silu(x):
    return x * jax.nn.sigmoid(x)


def _col_tile(n, cap=1536):
    best = 128
    for t in range(128, cap + 1, 128):
        if n % t == 0:
            best = t
    return best


def _mod_kernel(c_ref, w_ref, b_ref, o_ref):
    a = _silu(c_ref[...])
    o_ref[...] = _dot(a, w_ref[...]) + b_ref[...]


def _modulation(cvec, w_ada, b_ada):
    tn = 1024
    n = 6 * D_MODEL
    return pl.pallas_call(
        _mod_kernel,
        grid=(DEPTH, n // tn),
        in_specs=[
            pl.BlockSpec((8, D_MODEL), lambda l, j: (0, 0)),
            pl.BlockSpec((None, D_MODEL, tn), lambda l, j: (l, 0, j)),
            pl.BlockSpec((None, 1, tn), lambda l, j: (l, 0, j)),
        ],
        out_specs=pl.BlockSpec((None, 8, tn), lambda l, j: (l, 0, j)),
        out_shape=jax.ShapeDtypeStruct((DEPTH, 8, n), f32),
        compiler_params=_cparams(2),
        name="modulation",
    )(cvec, w_ada, b_ada.reshape(DEPTH, 1, n))


def _mod_spec(tm, tn, col_of):
    return pl.BlockSpec((None, 1, tn), lambda m, n: ((m * tm) // ROW_GROUP, 0, col_of(n)))


def _token_rows(x, tm, width, col_of, n_grid_axes):
    if n_grid_axes == 1:
        wrap = lambda f: (lambda m: f(m, 0))
    else:
        wrap = lambda f: f
    if not isinstance(x, tuple):
        return [pl.BlockSpec((tm, width), wrap(lambda m, c: (m, col_of(c))))], [x]
    n_ctx = N_CTX // tm
    ctx_map = lambda m, c: (jnp.minimum(m, n_ctx - 1), jnp.where(m < n_ctx, col_of(c), 0))
    lat_map = lambda m, c: (jnp.maximum(m - n_ctx, 0), jnp.where(m >= n_ctx, col_of(c), 0))
    return [pl.BlockSpec((tm, width), wrap(ctx_map)), pl.BlockSpec((tm, width), wrap(lat_map))], list(x)


def _read_rows(refs, tm):
    if len(refs) == 1:
        return refs[0][...]
    return jnp.where(pl.program_id(0) < N_CTX // tm, refs[0][...], refs[1][...])


NORM_TM = 512


def _norm_mod_kernel(*refs):
    g_ref, sc_ref, sh_ref, o_ref = refs[-4:]
    x = _read_rows(refs[:-4], NORM_TM)
    y = x * lax.rsqrt(jnp.mean(x * x, axis=-1, keepdims=True) + EPS) * g_ref[...]
    o_ref[...] = (y * (1.0 + sc_ref[...]) + sh_ref[...]).astype(o_ref.dtype)


def _norm_mod(x, g, scale, shift):
    tm = NORM_TM
    mod = pl.BlockSpec((None, 1, D_MODEL), lambda m: ((m * tm) // ROW_GROUP, 0, 0))
    x_specs, x_ops = _token_rows(x, tm, D_MODEL, lambda c: 0, 1)
    return pl.pallas_call(
        _norm_mod_kernel,
        grid=(N_TOK // tm,),
        in_specs=[*x_specs, pl.BlockSpec((1, D_MODEL), lambda m: (0, 0)), mod, mod],
        out_specs=pl.BlockSpec((tm, D_MODEL), lambda m: (m, 0)),
        out_shape=jax.ShapeDtypeStruct((N_TOK, D_MODEL), bf16),
        compiler_params=_cparams(1),
        name="norm_mod",
    )(*x_ops, g.reshape(1, D_MODEL), scale, shift)


LANE = 128


def _in_proj_kernel(x_ref, w_ref, wn_ref, o_ref, w_s, *, shift, n_valid):
    @pl.when(pl.program_id(1) == 0)
    def _():
        w = w_ref[...]
        if shift:
            w = jnp.concatenate([w[:, shift:], wn_ref[:, 0:shift]], axis=1)
        if n_valid < w.shape[1]:
            lane = lax.broadcasted_iota(jnp.int32, (1, w.shape[1]), 1)
            w = jnp.where(lane < n_valid, w, 0.0)
        w_s[...] = w.astype(bf16)

    o_ref[...] = jnp.dot(x_ref[...], w_s[...], preferred_element_type=f32).astype(o_ref.dtype)


def _in_proj(x, w_all, layer, col0, n, out_dtype, name):
    m, k = x.shape
    shift = col0 % LANE
    base = col0 - shift
    n_out = -(-n // LANE) * LANE
    tm = 1024
    tn = _col_tile(math.gcd(n_out, base) if base else n_out, 1024)
    j0 = base // tn
    per = tn // LANE
    return pl.pallas_call(
        functools.partial(_in_proj_kernel, shift=shift, n_valid=n if n < n_out else n_out),
        grid=(n_out // tn, m // tm),
        in_specs=[
            pl.BlockSpec((tm, k), lambda j, i: (i, 0)),
            pl.BlockSpec((None, k, tn), lambda j, i: (layer, 0, j0 + j)),
            pl.BlockSpec((None, k, LANE), lambda j, i: (layer, 0, (j0 + j + 1) * per)),
        ],
        out_specs=pl.BlockSpec((tm, tn), lambda j, i: (i, j)),
        out_shape=jax.ShapeDtypeStruct((m, n_out), out_dtype),
        scratch_shapes=[pltpu.VMEM((k, tn), bf16)],
        compiler_params=_cparams(2),
        name=name,
    )(x, w_all, w_all)


RES_TM = 1024


def _mm_res_kernel(a_ref, w_ref, g_ref, *refs):
    o_ref = refs[-1]
    y = jnp.dot(a_ref[...], w_ref[...], preferred_element_type=f32)
    o_ref[...] = _read_rows(refs[:-1], RES_TM) + g_ref[...] * y


def _matmul_residual(a, w, x, gate, name):
    m, k = a.shape
    n = w.shape[1]
    tm = RES_TM
    tn = 1024 if k <= D_MODEL else 512
    x_specs, x_ops = _token_rows(x, tm, tn, lambda j: j, 2)
    return pl.pallas_call(
        _mm_res_kernel,
        grid=(m // tm, n // tn),
        in_specs=[
            pl.BlockSpec((tm, k), lambda i, j: (i, 0)),
            pl.BlockSpec((k, tn), lambda i, j: (0, j)),
            _mod_spec(tm, tn, lambda j: j),
            *x_specs,
        ],
        out_specs=pl.BlockSpec((tm, tn), lambda i, j: (i, j)),
        out_shape=jax.ShapeDtypeStruct((m, n), f32),
        compiler_params=_cparams(2),
        name=name,
    )(a, w, gate, *x_ops)


MERGE_TM = 1024


def _merge_kernel(yp_ref, ygc_ref, ygl_ref, ync_ref, ynl_ref, wp_ref, wg_ref, wn_ref, g0_ref, g1_ref, g2_ref, o_ref):
    is_ctx = pl.program_id(0) < N_CTX // MERGE_TM

    def branch(y, w_ref, g_ref):
        return jax.nn.sigmoid(g_ref[...].astype(f32)) * jnp.dot(y, w_ref[...], preferred_element_type=f32)

    y_gdn = jnp.where(is_ctx, ygc_ref[...], ygl_ref[...])
    y_na = jnp.where(is_ctx, ync_ref[...], ynl_ref[...])
    acc = branch(yp_ref[...], wp_ref, g0_ref) + branch(y_gdn, wg_ref, g1_ref) + branch(y_na, wn_ref, g2_ref)
    o_ref[...] = acc.astype(o_ref.dtype)


def _merge(y_pool, y_gdn_ctx, y_gdn_lat, y_na_ctx, y_na_lat, w_pool, w_gdn, w_na, u_gate):
    tm = MERGE_TM
    tn = 1024
    nb = D_MODEL // tn
    n_ctx = N_CTX // tm
    row = lambda width: pl.BlockSpec((tm, width), lambda i, j: (i, 0))
    ctx = lambda width: pl.BlockSpec((tm, width), lambda i, j: (jnp.minimum(i, n_ctx - 1), 0))
    lat = lambda width: pl.BlockSpec((tm, width), lambda i, j: (jnp.maximum(i - n_ctx, 0), 0))
    wcol = lambda kdim: pl.BlockSpec((kdim, tn), lambda i, j: (0, j))
    gate = lambda b: pl.BlockSpec((tm, tn), lambda i, j: (i, j + b * nb))
    return pl.pallas_call(
        _merge_kernel,
        grid=(N_TOK // tm, nb),
        in_specs=[row(POOL_W), ctx(GDN_W), lat(GDN_W), ctx(NA_W), lat(NA_W),
                  wcol(POOL_W), wcol(GDN_W), wcol(NA_W), gate(0), gate(1), gate(2)],
        out_specs=pl.BlockSpec((tm, tn), lambda i, j: (i, j)),
        out_shape=jax.ShapeDtypeStruct((N_TOK, D_MODEL), bf16),
        compiler_params=_cparams(2),
        name="merge",
    )(y_pool, y_gdn_ctx, y_gdn_lat, y_na_ctx, y_na_lat, w_pool, w_gdn, w_na, u_gate, u_gate, u_gate)


def _tile_seq_len(m):
    return jnp.where(m * SEQ_TILE < N_CTX, SEQ, DEC_SEQ)


def _halo_specs(width, col_of):
    per = SEQ_TILE // HALO
    last = N_TOK // HALO - 1
    main = pl.BlockSpec((SEQ_TILE, width), lambda m, c: (m, col_of(c)))
    prev = pl.BlockSpec((HALO, width), lambda m, c: (jnp.maximum(m * per - 1, 0), col_of(c)))
    nxt = pl.BlockSpec((HALO, width), lambda m, c: (jnp.minimum((m + 1) * per, last), col_of(c)))
    return main, prev, nxt


def _tile_pos(m):
    t_len = _tile_seq_len(m)
    row = lax.broadcasted_iota(jnp.int32, (SEQ_TILE, 1), 0)
    return (m * SEQ_TILE + row) & (t_len - 1), t_len


def _pool_kernel(x_ref, xp_ref, xn_ref, w_ref, s_ref, o_ref, ext_ref):
    m = pl.program_id(0)
    pos, t_len = _tile_pos(m)
    first = ((m * SEQ_TILE) & (t_len - 1)) == 0
    last = (((m + 1) * SEQ_TILE) & (t_len - 1)) == 0
    x = x_ref[...].astype(f32)
    ext_ref[0:HALO, :] = jnp.where(first, 0.0, xp_ref[...].astype(f32))
    ext_ref[HALO:HALO + SEQ_TILE, :] = x
    ext_ref[HALO + SEQ_TILE:, :] = jnp.where(last, 0.0, xn_ref[...].astype(f32))
    outs = []
    for g, win in enumerate(POOL_WINDOWS):
        cols = slice(g * POOL_GC, (g + 1) * POOL_GC)
        acc = jnp.zeros((SEQ_TILE, POOL_GC), f32)
        for s in range(-(win // 2), win - win // 2):
            acc = acc + ext_ref[HALO + s:HALO + s + SEQ_TILE, cols]
        lo = jnp.maximum(pos - win // 2, 0)
        hi = jnp.minimum(pos + win - 1 - win // 2, t_len - 1)
        mean = acc / (hi - lo + 1).astype(f32)
        y = _dot(mean - x[:, cols], w_ref[g]) * s_ref[:, cols]
        outs.append(y)
    o_ref[...] = jnp.concatenate(outs, axis=1).astype(o_ref.dtype)


def _pool(u_a, pool_w, pool_scale):
    main, prev, nxt = _halo_specs(POOL_W, lambda c: 0)
    return pl.pallas_call(
        _pool_kernel,
        grid=(N_TOK // SEQ_TILE, 1),
        in_specs=[main, prev, nxt,
                  pl.BlockSpec((POOL_GROUPS, POOL_GC, POOL_GC), lambda m, c: (0, 0, 0)),
                  pl.BlockSpec((1, POOL_W), lambda m, c: (0, 0))],
        out_specs=pl.BlockSpec((SEQ_TILE, POOL_W), lambda m, c: (m, 0)),
        out_shape=jax.ShapeDtypeStruct((N_TOK, POOL_W), bf16),
        scratch_shapes=[pltpu.VMEM((SEQ_TILE + 2 * HALO, POOL_W), f32)],
        compiler_params=_cparams(2),
        name="pool",
    )(u_a, u_a, u_a, pool_w, pool_scale.reshape(1, POOL_W))


FFN_TM = 1024
FFN_TC = 512
FFN_SUB = 128


def _ffn_up_kernel(x_ref, xp_ref, xn_ref, wa_ref, wb_ref, ca_ref, cb_ref, o_ref, w_s):
    i = pl.program_id(1)
    n_sub = FFN_TC // FFN_SUB
    sub = FFN_SUB

    @pl.when(i == 0)
    def _():
        for s in range(n_sub):
            w_s[:, 2 * s * sub:(2 * s + 1) * sub] = wa_ref[:, s * sub:(s + 1) * sub].astype(bf16)
            w_s[:, (2 * s + 1) * sub:(2 * s + 2) * sub] = wb_ref[:, s * sub:(s + 1) * sub].astype(bf16)

    rows = FFN_TM + 2 * HALO
    x_ext = jnp.concatenate([xp_ref[...], x_ref[...], xn_ref[...]], axis=0)
    t_len = jnp.where(i * FFN_TM < N_CTX, SEQ, DEC_SEQ)
    pos = (i * FFN_TM + lax.broadcasted_iota(jnp.int32, (FFN_TM, 1), 0)) & (t_len - 1)
    first = pos == 0
    last = pos == t_len - 1
    for s in range(n_sub):
        ab = jnp.dot(x_ext, w_s[:, 2 * s * sub:(2 * s + 2) * sub], preferred_element_type=f32)
        taps = jnp.concatenate([ca_ref[:, s * sub:(s + 1) * sub], cb_ref[:, s * sub:(s + 1) * sub]], axis=1)
        before = jnp.where(first, 0.0, pltpu.roll(ab, 1, 0)[HALO:HALO + FFN_TM])
        after = jnp.where(last, 0.0, pltpu.roll(ab, rows - 1, 0)[HALO:HALO + FFN_TM])
        y = before * taps[0:1] + ab[HALO:HALO + FFN_TM] * taps[1:2] + after * taps[2:3]
        o_ref[:, s * sub:(s + 1) * sub] = (_silu(y[:, 0:sub]) * y[:, sub:]).astype(o_ref.dtype)


def _ffn_up_glu(h, w_up_all, layer, ffn_conv):
    nb = D_FF // FFN_TC
    per = FFN_TM // HALO
    last = N_TOK // HALO - 1
    return pl.pallas_call(
        _ffn_up_kernel,
        grid=(nb, N_TOK // FFN_TM),
        in_specs=[pl.BlockSpec((FFN_TM, D_MODEL), lambda j, i: (i, 0)),
                  pl.BlockSpec((HALO, D_MODEL), lambda j, i: (jnp.maximum(i * per - 1, 0), 0)),
                  pl.BlockSpec((HALO, D_MODEL), lambda j, i: (jnp.minimum((i + 1) * per, last), 0)),
                  pl.BlockSpec((None, D_MODEL, FFN_TC), lambda j, i: (layer, 0, j)),
                  pl.BlockSpec((None, D_MODEL, FFN_TC), lambda j, i: (layer, 0, j + nb)),
                  pl.BlockSpec((3, FFN_TC), lambda j, i: (0, j)),
                  pl.BlockSpec((3, FFN_TC), lambda j, i: (0, j + nb))],
        out_specs=pl.BlockSpec((FFN_TM, FFN_TC), lambda j, i: (i, j)),
        out_shape=jax.ShapeDtypeStruct((N_TOK, D_FF), bf16),
        scratch_shapes=[pltpu.VMEM((D_MODEL, 2 * FFN_TC), bf16)],
        compiler_params=_cparams(2),
        name="ffn_up_glu",
    )(h, h, h, w_up_all, w_up_all, ffn_conv, ffn_conv)


GATE_LANES = 4 * GDN_HEADS
CHUNKS_PER_TILE = SEQ_TILE // GDN_CHUNK


def _gate_prep_kernel(u_ref, alog_ref, dtb_ref, cols_ref, rows_ref):
    u = u_ref[...]
    lane = lax.broadcasted_iota(jnp.int32, (1, B_W), 1)
    x = u + dtb_ref[...]
    softplus = jnp.maximum(x, 0.0) + jnp.log1p(jnp.exp(-jnp.abs(x)))
    g = -jnp.exp(alog_ref[...]) * softplus
    r = lax.broadcasted_iota(jnp.int32, (SEQ_TILE, SEQ_TILE), 0)
    c = lax.broadcasted_iota(jnp.int32, (SEQ_TILE, SEQ_TILE), 1)
    shift = int(math.log2(GDN_CHUNK))
    same = (r >> shift) == (c >> shift)
    tri_f = jnp.where(same & (c <= r), 1.0, 0.0)
    tri_b = jnp.where(same & (c >= r), 1.0, 0.0)
    gc_f = jnp.dot(tri_f, g, preferred_element_type=f32, precision=lax.Precision.HIGHEST)
    gc_b = jnp.dot(tri_b, g, preferred_element_type=f32, precision=lax.Precision.HIGHEST)
    cols = jnp.where(lane < 2 * GDN_HEADS, jax.nn.sigmoid(u), jnp.where(lane < 3 * GDN_HEADS, gc_f, gc_b))
    cols_ref[...] = cols
    for j in range(CHUNKS_PER_TILE):
        t = cols[j * GDN_CHUNK:(j + 1) * GDN_CHUNK, :].T
        rows_ref[j] = t[0:GATE_LANES, :]


def _gate_prep(u_b, a_log, dt_bias):
    pad = lambda p: jnp.zeros((1, B_W), f32).at[0, 2 * GDN_HEADS:GATE_LANES].set(p.reshape(-1))
    n_tiles = N_TOK // SEQ_TILE
    return pl.pallas_call(
        _gate_prep_kernel,
        grid=(n_tiles,),
        in_specs=[pl.BlockSpec((SEQ_TILE, B_W), lambda m: (m, 0)),
                  pl.BlockSpec((1, B_W), lambda m: (0, 0)),
                  pl.BlockSpec((1, B_W), lambda m: (0, 0))],
        out_specs=[pl.BlockSpec((SEQ_TILE, B_W), lambda m: (m, 0)),
                   pl.BlockSpec((CHUNKS_PER_TILE, GATE_LANES, GDN_CHUNK), lambda m: (m, 0, 0))],
        out_shape=[jax.ShapeDtypeStruct((N_TOK, B_W), f32),
                   jax.ShapeDtypeStruct((N_TOK // GDN_CHUNK, GATE_LANES, GDN_CHUNK), f32)],
        compiler_params=_cparams(1),
        name="gdn_gate_prep",
    )(u_b, pad(a_log), pad(dt_bias))


def _gdn_conv_silu(x_ref, w_ref, t_len):
    x = x_ref[...].astype(f32)
    row = lax.broadcasted_iota(jnp.int32, (t_len, 1), 0)
    before = jnp.where(row == 0, 0.0, pltpu.roll(x, 1, 0))
    after = jnp.where(row == t_len - 1, 0.0, pltpu.roll(x, t_len - 1, 0))
    return _silu(before * w_ref[0:1, :] + x * w_ref[1:2, :] + after * w_ref[2:3, :])


def _l2norm_heads(x, n_heads):
    hd = GDN_HEAD_DIM
    parts = []
    for h in range(n_heads):
        xh = x[:, h * hd:(h + 1) * hd]
        parts.append(xh * lax.rsqrt(jnp.sum(xh * xh, axis=-1, keepdims=True) + EPS))
    return parts[0] if n_heads == 1 else jnp.concatenate(parts, axis=1)


GDN_STEP_ROWS = GDN_HEAD_DIM + GDN_CHUNK


def _gdn_prepare(grp, head0, q_s, k_s, v_s, gcol_ref, grow_ref, pq_s, b_s, gl_s, o_s, *, group, hp):
    cs, hd = GDN_CHUNK, GDN_HEAD_DIM
    rows = group * cs
    r0 = pl.multiple_of(grp * rows, rows)
    lane = lax.broadcasted_iota(jnp.int32, (1, B_W), 1)
    ri = lax.broadcasted_iota(jnp.int32, (cs, cs), 0)
    ci = lax.broadcasted_iota(jnp.int32, (cs, cs), 1)
    blk = ri ^ ci
    probs = []
    for j in range(group):
        rj = r0 + j * cs
        gates = gcol_ref[pl.ds(rj, cs), :]
        pick = lambda idx, gates=gates: jnp.sum(jnp.where(lane == idx, gates, 0.0), axis=1, keepdims=True)
        for hh in range(hp):
            cols = slice(hh * hd, (hh + 1) * hd)
            head = head0 + hh
            q = q_s[pl.ds(rj, cs), cols]
            k = k_s[pl.ds(rj, cs), cols]
            v = v_s[pl.ds(rj, cs), cols]
            kk = _dot_nt(k, k)
            qk = _dot_nt(q, k)
            for direction in range(2):
                beta = pick(direction * GDN_HEADS + head)
                dec_lane = (2 + direction) * GDN_HEADS + head
                gc = pick(dec_lane)
                gr = grow_ref[grp * group + j, pl.ds(dec_lane, 1), :]
                if direction == 0:
                    incl, strict = ri >= ci, ri > ci
                    g_tot = gc[cs - 1:cs, :]
                else:
                    incl, strict = ri <= ci, ri < ci
                    g_tot = gc[0:1, :]
                decay = jnp.exp(jnp.where(incl, gc - gr, -jnp.inf))
                probs.append(dict(j=j, hh=hh, d=direction, beta=beta, gc=gc, g_tot=g_tot,
                                  qk=qk * decay, lmat=jnp.where(strict, kk * decay * beta, 0.0)))
    for p in probs:
        p["n"] = -jnp.where(blk < 2, p["lmat"], 0.0)
    for level in range(1, int(math.log2(cs))):
        mask = (blk >= (1 << level)) & (blk < (2 << level))
        for p in probs:
            c_k = jnp.where(mask, p["lmat"], 0.0)
            y = c_k + _dot(c_k, p["n"])
            p["n"] = p["n"] - y - _dot(p["n"], y)
    for p in probs:
        j, hh, d = p["j"], p["hh"], p["d"]
        cols = slice(hh * hd, (hh + 1) * hd)
        q = q_s[pl.ds(r0 + j * cs, cs), cols]
        k = k_s[pl.ds(r0 + j * cs, cs), cols]
        v = v_s[pl.ds(r0 + j * cs, cs), cols]
        e_gc = jnp.exp(p["gc"])
        vb = v * p["beta"]
        kbg = k * (p["beta"] * e_gc)
        uw = _dot(p["n"], jnp.concatenate([vb, kbg], axis=1))
        wu = jnp.concatenate([kbg + uw[:, hd:], vb + uw[:, 0:hd]], axis=1)
        kd = k * jnp.exp(p["g_tot"] - p["gc"])
        top = _dot_tn(kd, wu)
        bot = _dot(p["qk"], wu)
        c = grp * group + j
        pq_s[hh, d, c, 0:hd, :] = top[:, 0:hd].astype(pq_s.dtype)
        pq_s[hh, d, c, hd:, :] = (q * e_gc - bot[:, 0:hd]).astype(pq_s.dtype)
        b_s[hh, d, c] = top[:, hd:]
        o_s[hh, d, pl.ds(r0 + j * cs, cs), :] = bot[:, hd:]
        gl_s[hh, d, c] = jnp.broadcast_to(jnp.exp(p["g_tot"]), (1, hd))


def _gdn_scan_step(c, hh, d, state, pq_s, b_s, gl_s, o_s):
    hd = GDN_HEAD_DIM
    r0 = pl.multiple_of(c * GDN_CHUNK, GDN_CHUNK)
    r = _dot(pq_s[hh, d, c], state)
    o_s[hh, d, pl.ds(r0, GDN_CHUNK), :] += r[hd:]
    return state * gl_s[hh, d, c] - r[0:hd] + b_s[hh, d, c]


def _gdn_kernel(q_ref, k_ref, v_ref, z_ref, wq_ref, wk_ref, wv_ref, gcol_ref, grow_ref, ng_ref, s0_ref,
                y_ref, sfin_ref, q_s, k_s, v_s, o_s, pq_s, b_s, gl_s, *, t_len, group, hp):
    hd = GDN_HEAD_DIM
    head0 = pl.program_id(1) * hp
    n_chunks = t_len // GDN_CHUNK
    q_s[...] = _l2norm_heads(_gdn_conv_silu(q_ref, wq_ref, t_len), hp) * (hd ** -0.5)
    k_s[...] = _l2norm_heads(_gdn_conv_silu(k_ref, wk_ref, t_len), hp)
    v_s[...] = _gdn_conv_silu(v_ref, wv_ref, t_len)

    def prepare(grp, carry):
        _gdn_prepare(grp, head0, q_s, k_s, v_s, gcol_ref, grow_ref, pq_s, b_s, gl_s, o_s, group=group, hp=hp)
        return carry

    lax.fori_loop(0, n_chunks // group, prepare, 0)

    def scan(i, states):
        new = []
        for hh in range(hp):
            new.append(_gdn_scan_step(i, hh, 0, states[2 * hh], pq_s, b_s, gl_s, o_s))
            new.append(_gdn_scan_step(n_chunks - 1 - i, hh, 1, states[2 * hh + 1], pq_s, b_s, gl_s, o_s))
        return tuple(new)

    init = tuple(s0_ref[d, hh] for hh in range(hp) for d in range(2))
    final = lax.fori_loop(0, n_chunks, scan, init)
    outs = []
    for hh in range(hp):
        sfin_ref[0, hh] = final[2 * hh]
        sfin_ref[1, hh] = final[2 * hh + 1]
        o = o_s[hh, 0] + o_s[hh, 1]
        outs.append(o * lax.rsqrt(jnp.mean(o * o, axis=-1, keepdims=True) + EPS) * ng_ref[...])
    o = outs[0] if hp == 1 else jnp.concatenate(outs, axis=1)
    y_ref[...] = (o * _silu(z_ref[...].astype(f32))).astype(y_ref.dtype)


def _gdn(u_a, gcols, grows, conv_w, norm_g, s0, t_len, n_seq, row_off, hp, group):
    hd = GDN_HEAD_DIM
    wd = hd * hp
    nb = GDN_HEADS // hp
    blk0 = row_off // t_len
    c_q = POOL_W // wd
    col = lambda base: pl.BlockSpec((t_len, wd), lambda s, h: (blk0 + s, base + h))
    cw = lambda base: pl.BlockSpec((3, wd), lambda s, h: (0, base + h))
    n_chunks = t_len // GDN_CHUNK
    state_spec = pl.BlockSpec((None, 2, hp, hd, hd), lambda s, h: (s, 0, h, 0, 0))
    return pl.pallas_call(
        functools.partial(_gdn_kernel, t_len=t_len, group=group, hp=hp),
        grid=(n_seq, nb),
        in_specs=[col(c_q), col(c_q + nb), col(c_q + 2 * nb), col(c_q + 3 * nb),
                  cw(0), cw(nb), cw(2 * nb),
                  pl.BlockSpec((t_len, B_W), lambda s, h: (blk0 + s, 0)),
                  pl.BlockSpec((n_chunks, GATE_LANES, GDN_CHUNK), lambda s, h: (blk0 + s, 0, 0)),
                  pl.BlockSpec((1, hd), lambda s, h: (0, 0)),
                  state_spec],
        out_specs=[pl.BlockSpec((t_len, wd), lambda s, h: (s, h)), state_spec],
        out_shape=[jax.ShapeDtypeStruct((n_seq * t_len, GDN_W), bf16),
                   jax.ShapeDtypeStruct((n_seq, 2, GDN_HEADS, hd, hd), f32)],
        scratch_shapes=[pltpu.VMEM((t_len, wd), f32)] * 3
        + [pltpu.VMEM((hp, 2, t_len, hd), f32),
           pltpu.VMEM((hp, 2, n_chunks, GDN_STEP_ROWS, hd), bf16),
           pltpu.VMEM((hp, 2, n_chunks, hd, hd), f32),
           pltpu.VMEM((hp, 2, n_chunks, 1, hd), f32)],
        compiler_params=_cparams(2),
        name=f"gdn_t{t_len}",
    )(u_a, u_a, u_a, u_a, conv_w, conv_w, conv_w, gcols, grows, norm_g.reshape(1, hd), s0)


def _ctx_attn_kernel(x_ref, o_ref):
    hd = NA_HEAD_DIM
    outs = []
    for h in range(NA_HEADS):
        q = x_ref[:, h * hd:(h + 1) * hd]
        k = x_ref[:, NA_W + h * hd:NA_W + (h + 1) * hd]
        v = x_ref[:, 2 * NA_W + h * hd:2 * NA_W + (h + 1) * hd]
        s = _dot_nt(q, k) * (hd ** -0.5)
        p = jnp.exp(s - jnp.max(s, axis=-1, keepdims=True))
        outs.append(_dot(p, v) / jnp.sum(p, axis=-1, keepdims=True))
    o_ref[...] = jnp.concatenate(outs, axis=1).astype(o_ref.dtype)


def _ctx_attention(u_c):
    return pl.pallas_call(
        _ctx_attn_kernel,
        grid=(BATCH,),
        in_specs=[pl.BlockSpec((SEQ, C_W), lambda b: (b, 0))],
        out_specs=pl.BlockSpec((SEQ, NA_W), lambda b: (b, 0)),
        out_shape=jax.ShapeDtypeStruct((N_CTX, NA_W), bf16),
        compiler_params=_cparams(1),
        name="ctx_attention",
    )(u_c)


HEADS_PER_STEP = 128 // NA_HEAD_DIM
NB_KEYS = NA_WR * GRID_W


def _na_kernel(q_ref, k_ref, v_ref, kc_ref, vc_ref, bias_ref, o_ref, k_s, v_s, kc_s, vc_s):
    hd = NA_HEAD_DIM
    scale = hd ** -0.5
    k_s[...] = k_ref[...].astype(bf16)
    v_s[...] = v_ref[...].astype(bf16)
    kc_s[...] = kc_ref[...].astype(bf16)
    vc_s[...] = vc_ref[...].astype(bf16)
    lane = lax.broadcasted_iota(jnp.int32, (1, HEADS_PER_STEP * hd), 1)

    def body(r, carry):
        start = jnp.clip(r - NA_WR // 2, 0, GRID_ROWS - NA_WR)
        case = start - r + NA_WR - 1
        q0 = pl.multiple_of(r * GRID_W, GRID_W)
        k0 = pl.multiple_of(start * GRID_W, GRID_W)
        q = q_ref[pl.ds(q0, GRID_W), :] * scale
        k_nb = k_s[pl.ds(k0, NB_KEYS), :]
        v_nb = v_s[pl.ds(k0, NB_KEYS), :]
        own = [(lane >= hh * hd) & (lane < (hh + 1) * hd) for hh in range(HEADS_PER_STEP)]
        qh = jnp.concatenate([jnp.where(m, q, 0.0) for m in own], axis=0).astype(bf16)
        bias = bias_ref[:, case].reshape(HEADS_PER_STEP * GRID_W, NB_KEYS)
        s_nb = _dot_nt(qh, k_nb) + bias
        s_cx = _dot_nt(qh, kc_s[...])
        mx = jnp.maximum(jnp.max(s_nb, axis=-1, keepdims=True), jnp.max(s_cx, axis=-1, keepdims=True))
        p_nb = jnp.exp(s_nb - mx)
        p_cx = jnp.exp(s_cx - mx)
        den = jnp.sum(p_nb, axis=-1, keepdims=True) + jnp.sum(p_cx, axis=-1, keepdims=True)
        o = (_dot(p_nb, v_nb) + _dot(p_cx, vc_s[...])) / den
        out = o[0:GRID_W]
        for hh in range(1, HEADS_PER_STEP):
            out = jnp.where(own[hh], o[hh * GRID_W:(hh + 1) * GRID_W], out)
        o_ref[pl.ds(q0, GRID_W), :] = out.astype(o_ref.dtype)
        return carry

    lax.fori_loop(0, GRID_ROWS, body, 0, unroll=2)


def _na_bias_tiles(rpb):
    col = np.arange(GRID_W)
    col_start = np.clip(col - NA_WC // 2, 0, GRID_W - NA_WC)
    col_mask = (col[None, :] >= col_start[:, None]) & (col[None, :] < col_start[:, None] + NA_WC)
    d_col = np.clip(col[None, :] - col[:, None], -(NA_WC - 1), NA_WC - 1) + NA_WC - 1
    onehot = (d_col[:, :, None] == np.arange(2 * NA_WC - 1)).astype(np.float32)
    picked = jnp.einsum("hrd,qkd->hrqk", rpb, onehot, precision=lax.Precision.HIGHEST)
    table = jnp.where(col_mask[None, None], picked, NEG_BIG)
    tiles = [table[:, case:case + NA_WR].transpose(0, 2, 1, 3).reshape(NA_HEADS, GRID_W, NB_KEYS)
             for case in range(NA_WR)]
    return jnp.stack(tiles, axis=1).astype(f32)


def _na_attention(u_c, k_ctx, v_ctx, bias_tiles):
    blk0 = N_CTX // DEC_SEQ
    nhp = NA_HEADS // HEADS_PER_STEP
    col = lambda base: pl.BlockSpec((DEC_SEQ, 128), lambda b, p: (blk0 + b, base + p))
    cache = pl.BlockSpec((None, PAST_LEN, 128), lambda b, p: (b, 0, p))
    return pl.pallas_call(
        _na_kernel,
        grid=(DEC_BATCH, nhp),
        in_specs=[col(0), col(nhp), col(2 * nhp), cache, cache,
                  pl.BlockSpec((HEADS_PER_STEP, NA_WR, GRID_W, NB_KEYS), lambda b, p: (p, 0, 0, 0))],
        out_specs=pl.BlockSpec((DEC_SEQ, 128), lambda b, p: (b, p)),
        out_shape=jax.ShapeDtypeStruct((N_LAT, NA_W), bf16),
        scratch_shapes=[pltpu.VMEM((DEC_SEQ, 128), bf16)] * 2 + [pltpu.VMEM((PAST_LEN, 128), bf16)] * 2,
        compiler_params=_cparams(2),
        name="na_attention",
    )(u_c, u_c, u_c, k_ctx, v_ctx, bias_tiles)


def _final_norm_kernel(x_ref, g_ref, o_ref):
    x = x_ref[...]
    o_ref[...] = x * lax.rsqrt(jnp.mean(x * x, axis=-1, keepdims=True) + EPS) * g_ref[...]


def _final_norm(x, g, n_rows, row_off):
    tm = 512
    blk0 = row_off // tm
    return pl.pallas_call(
        _final_norm_kernel,
        grid=(n_rows // tm,),
        in_specs=[pl.BlockSpec((tm, D_MODEL), lambda m: (blk0 + m, 0)),
                  pl.BlockSpec((1, D_MODEL), lambda m: (0, 0))],
        out_specs=pl.BlockSpec((tm, D_MODEL), lambda m: (m, 0)),
        out_shape=jax.ShapeDtypeStruct((n_rows, D_MODEL), f32),
        compiler_params=_cparams(1),
        name="final_norm",
    )(x, g.reshape(1, D_MODEL))


def kernel(x_prompt, x_sample, cache_na_k, cache_na_v, state_gdn, c, c_ctx, w_ada, b_ada, g_norm1, w_in, pool_w, pool_scale, gdn_conv, gdn_a_log, gdn_dt_bias, gdn_norm_g, na_rpb, w_branch_pool, w_branch_gdn, w_branch_na, w_out, g_norm2, w_up, ffn_conv, w_down, g_final):
    x = (x_prompt.reshape(N_CTX, D_MODEL), x_sample.reshape(N_LAT, D_MODEL))

    cvec = jnp.zeros((8, D_MODEL), f32).at[0].set(c_ctx).at[1:1 + DEC_BATCH].set(c)
    mods = _modulation(cvec, w_ada, b_ada)
    group_row = np.array([0] * (N_CTX // ROW_GROUP) + list(range(1, 1 + DEC_BATCH)))
    mods = mods[:, group_row].reshape(DEPTH, N_GROUPS, 6, 1, D_MODEL).transpose(0, 2, 1, 3, 4)

    w_in_bf = w_in.astype(bf16)
    zero_state = jnp.zeros((BATCH, 2, GDN_HEADS, GDN_HEAD_DIM, GDN_HEAD_DIM), f32)
    new_k, new_v, new_s = [], [], []
    for l in range(DEPTH):
        sh1, sc1, g1, sh2, sc2, g2 = (mods[l, j] for j in range(6))
        h = _norm_mod(x, g_norm1[l], sc1, sh1)
        u_a = _in_proj(h, w_in_bf, l, 0, A_W, bf16, "in_proj_mix")
        u_b = _in_proj(h, w_in_bf, l, A_W, GATE_LANES, f32, "in_proj_gates")
        u_c = _in_proj(h, w_in_bf, l, A_W + GATE_LANES, C_W, f32, "in_proj_attn")
        u_g = _in_proj(h, w_in_bf, l, A_W + GATE_LANES + C_W, G_W, bf16, "in_proj_branch_gates")

        y_pool = _pool(u_a, pool_w[l].astype(bf16), pool_scale[l])

        gcols, grows = _gate_prep(u_b, gdn_a_log[l], gdn_dt_bias[l])
        y_gdn_ctx, s_ctx = _gdn(u_a, gcols, grows, gdn_conv[l], gdn_norm_g[l], zero_state, SEQ, BATCH, 0,
                                hp=4, group=SEQ // GDN_CHUNK)
        y_gdn_lat, _ = _gdn(u_a, gcols, grows, gdn_conv[l], gdn_norm_g[l], state_gdn[:, l], DEC_SEQ, DEC_BATCH,
                            N_CTX, hp=2, group=8)

        y_na_ctx = _ctx_attention(u_c)
        y_na_lat = _na_attention(u_c, cache_na_k[:, l].reshape(DEC_BATCH, PAST_LEN, NA_W),
                                 cache_na_v[:, l].reshape(DEC_BATCH, PAST_LEN, NA_W), _na_bias_tiles(na_rpb[l]))

        merged = _merge(y_pool, y_gdn_ctx, y_gdn_lat, y_na_ctx, y_na_lat, w_branch_pool[l].astype(bf16),
                        w_branch_gdn[l].astype(bf16), w_branch_na[l].astype(bf16), u_g)
        x = _matmul_residual(merged, w_out[l].astype(bf16), x, g1, "out_proj")

        h = _norm_mod(x, g_norm2[l], sc2, sh2)
        act = _ffn_up_glu(h, w_up, l, ffn_conv[l])
        x = _matmul_residual(act, w_down[l].astype(bf16), x, g2, "ffn_down")

        new_k.append(u_c[:N_CTX, NA_W:2 * NA_W].reshape(BATCH, SEQ, NA_HEADS, NA_HEAD_DIM))
        new_v.append(u_c[:N_CTX, 2 * NA_W:].reshape(BATCH, SEQ, NA_HEADS, NA_HEAD_DIM))
        new_s.append(s_ctx)

    y_prompt = _final_norm(x, g_final, N_CTX, 0).reshape(BATCH, SEQ, D_MODEL)
    y_sample = _final_norm(x, g_final, N_LAT, N_CTX).reshape(DEC_BATCH, DEC_SEQ, D_MODEL)
    return (y_prompt, y_sample, jnp.stack(new_k, axis=1), jnp.stack(new_v, axis=1), jnp.stack(new_s, axis=1))
```
